```python
import math
import jax, jax.numpy as jnp
from jax import lax
import numpy as np


D_MODEL = 1024
BATCH = 8
SEQ = 8192
DEPTH = 2

A_HEAD_DIM = 64
A_WIDTH = D_MODEL // 2
A_HEADS = A_WIDTH // A_HEAD_DIM
A_DECAY_LORA = 32
A_ICLR_LORA = 32
A_COLS = 4 * A_WIDTH + A_DECAY_LORA + A_ICLR_LORA
A_GN_EPS = 64e-5

B_HEAD_DIM = 64
B_WIDTH = D_MODEL // 2
B_HEADS = B_WIDTH // B_HEAD_DIM
B_KV_HEADS = 2
B_GROUP = B_HEADS // B_KV_HEADS
B_KV_WIDTH = B_KV_HEADS * B_HEAD_DIM
B_COLS = 2 * B_WIDTH + 2 * B_KV_WIDTH
WINDOW = 128
BLOCK = 128

REL_BUCKETS = 32
REL_MAX_DIST = 128

C_KEY_DIM = 128
C_VAL_DIM = 128
C_HEADS = D_MODEL // C_KEY_DIM
C_WIDTH = C_HEADS * C_VAL_DIM
C_COLS = 4 * C_WIDTH
CHUNK = 64

NORM_EPS = 1e-6
NEG_INF = -1e30

kernel_name = 'hybrid_rwkv7_swa_hgrn2_trunk'


def _split(x, sizes):
    idx = [int(i) for i in np.cumsum(sizes)[:-1]]
    return jnp.split(x, idx, axis=-1)


def rms_norm(x, w):
    xf = x.astype(jnp.float32)
    y = xf * lax.rsqrt(jnp.mean(xf * xf, axis=-1, keepdims=True) + NORM_EPS)
    return y * w.astype(jnp.float32)


def token_shift(feats, mu):
    prev = jnp.pad(feats, ((0, 0), (1, 0), (0, 0)))[:, :-1]
    return feats + mu * (prev - feats)


def t5_bucket(dist):
    max_exact = REL_BUCKETS // 2
    d = jnp.maximum(dist, 0)
    large = max_exact + (jnp.log(jnp.maximum(d, 1).astype(jnp.float32) / max_exact)
                         / math.log(REL_MAX_DIST / max_exact)
                         * (REL_BUCKETS - max_exact)).astype(jnp.int32)
    large = jnp.minimum(large, REL_BUCKETS - 1)
    return jnp.where(d < max_exact, d, large)


def rwkv7_scan(r, decay, k, v, kk, a):
    bsz, _, heads, n = r.shape
    xs = tuple(jnp.moveaxis(z, 1, 0) for z in (r, decay, k, v, kk, a))

    def step(state, inp):
        r_t, w_t, k_t, v_t, kk_t, a_t = inp
        sa = jnp.einsum('bhvk,bhk->bhv', state, -kk_t)
        state = (state * w_t[:, :, None, :]
                 + sa[..., None] * (kk_t * a_t)[:, :, None, :]
                 + v_t[..., None] * k_t[:, :, None, :])
        return state, jnp.einsum('bhvk,bhk->bhv', state, r_t)

    s0 = jnp.zeros((bsz, heads, n, n), jnp.float32)
    _, ys = lax.scan(step, s0, xs)
    return jnp.moveaxis(ys, 0, 1)


def rwkv7_branch(feats, a_w0, a_w2, a_a0, a_a2, a_k_k, a_k_a, a_r_k, a_ln_w, a_ln_b):
    bsz, t = feats.shape[:2]
    r, k, v, w_lr, a_lr = _split(feats, [A_WIDTH, A_WIDTH, A_WIDTH, A_DECAY_LORA, A_ICLR_LORA])
    w = -jax.nn.softplus(-(a_w0 + jnp.tanh(w_lr) @ a_w2)) - 0.5
    decay = jnp.exp(-jnp.exp(w))
    a = jax.nn.sigmoid(a_a0 + a_lr @ a_a2)
    kk = k * a_k_k
    k = k * (1.0 + (a - 1.0) * a_k_a)

    def hs(z):
        return z.reshape(bsz, t, A_HEADS, A_HEAD_DIM)

    kk = hs(kk)
    kk = kk / jnp.maximum(jnp.sqrt(jnp.sum(kk * kk, axis=-1, keepdims=True)), 1e-12)
    r, k, v, decay, a = hs(r), hs(k), hs(v), hs(decay), hs(a)
    y = rwkv7_scan(r, decay, k, v, kk, a)
    mean = jnp.mean(y, axis=-1, keepdims=True)
    var = jnp.mean(jnp.square(y - mean), axis=-1, keepdims=True)
    y = ((y - mean) * lax.rsqrt(var + A_GN_EPS) * a_ln_w.reshape(A_HEADS, A_HEAD_DIM)
         + a_ln_b.reshape(A_HEADS, A_HEAD_DIM))
    y = y + jnp.sum(r * k * a_r_k, axis=-1, keepdims=True) * v
    return y.reshape(bsz, t, A_WIDTH)


def swa_sink_attention(q, k, v, sinks, rel_bias_table):
    bsz, t = q.shape[:2]
    nb = t // BLOCK
    q = q.reshape(bsz, nb, BLOCK, B_KV_HEADS, B_GROUP, B_HEAD_DIM)
    pad = ((0, 0), (BLOCK, 0), (0, 0), (0, 0))
    kp = jnp.pad(k.reshape(bsz, t, B_KV_HEADS, B_HEAD_DIM), pad).reshape(bsz, nb + 1, BLOCK, B_KV_HEADS, B_HEAD_DIM)
    vp = jnp.pad(v.reshape(bsz, t, B_KV_HEADS, B_HEAD_DIM), pad).reshape(bsz, nb + 1, BLOCK, B_KV_HEADS, B_HEAD_DIM)
    kb = jnp.concatenate([kp[:, :-1], kp[:, 1:]], axis=2)
    vb = jnp.concatenate([vp[:, :-1], vp[:, 1:]], axis=2)
    scores = jnp.einsum('bnqhgd,bnkhd->bnhgqk', q, kb) * (B_HEAD_DIM ** -0.5)

    qi = jnp.arange(BLOCK)[:, None]
    kj = jnp.arange(2 * BLOCK)[None, :]
    dist = qi + BLOCK - kj
    bias = rel_bias_table[t5_bucket(dist)]
    bias = jnp.transpose(bias, (2, 0, 1)).reshape(B_KV_HEADS, B_GROUP, BLOCK, 2 * BLOCK)
    in_window = (dist >= 0) & (dist < WINDOW)
    key_pos = jnp.arange(nb)[:, None, None] * BLOCK - BLOCK + kj[None]
    valid = in_window[None] & (key_pos >= 0)
    scores = jnp.where(valid[None, :, None, None], scores + bias, NEG_INF)

    sink = sinks.reshape(B_KV_HEADS, B_GROUP)[None, None, :, :, None, None]
    m = jnp.maximum(jnp.max(scores, axis=-1, keepdims=True), sink)
    p = jnp.exp(scores - m)
    denom = jnp.sum(p, axis=-1, keepdims=True) + jnp.exp(sink - m)
    out = jnp.einsum('bnhgqk,bnkhd->bnqhgd', p / denom, vb)
    return out.reshape(bsz, t, B_WIDTH)


def rwkv_swa_layer(h, w_in, w_out, a_mu, a_w0, a_w2, a_a0, a_a2, a_k_k, a_k_a, a_r_k,
                   a_ln_w, a_ln_b, b_sinks, rel_bias_table):
    proj = h @ w_in
    a_feats = token_shift(proj[..., :A_COLS], a_mu)
    a_core, a_gate = a_feats[..., :-A_WIDTH], a_feats[..., -A_WIDTH:]
    q_b, k_b, v_b, b_gate = _split(proj[..., A_COLS:], [B_WIDTH, B_KV_WIDTH, B_KV_WIDTH, B_WIDTH])
    y_a = rwkv7_branch(a_core, a_w0, a_w2, a_a0, a_a2, a_k_k, a_k_a, a_r_k, a_ln_w, a_ln_b)
    y_b = swa_sink_attention(q_b, k_b, v_b, b_sinks, rel_bias_table)
    y = jnp.concatenate([y_a * jax.nn.silu(a_gate), y_b * jax.nn.silu(b_gate)], axis=-1)
    return y @ w_out


def hgrn2_chunkwise(q, k, v, log_f):
    bsz, t, heads, dk = q.shape
    dv = v.shape[-1]
    nc = t // CHUNK

    def chunked(z):
        return jnp.moveaxis(z.reshape(bsz, nc, CHUNK, heads, z.shape[-1]), 1, 0)

    causal = jnp.tril(jnp.ones((CHUNK, CHUNK), dtype=bool))

    def step(state, inp):
        q_c, k_c, v_c, g_c = inp
        g_cum = jnp.cumsum(g_c, axis=1)
        g_last = g_cum[:, -1]
        q_dec = q_c * jnp.exp(g_cum)
        k_inv = k_c * jnp.exp(-g_cum)
        scores = jnp.where(causal, jnp.einsum('bihk,bjhk->bhij', q_dec, k_inv), 0.0)
        o = (jnp.einsum('bhij,bjhv->bihv', scores, v_c)
             + jnp.einsum('bihk,bhkv->bihv', q_dec, state))
        k_tail = k_c * jnp.exp(g_last[:, None] - g_cum)
        state = state * jnp.exp(g_last)[..., None] + jnp.einsum('bjhk,bjhv->bhkv', k_tail, v_c)
        return state, o

    s0 = jnp.zeros((bsz, heads, dk, dv), jnp.float32)
    _, o = lax.scan(step, s0, (chunked(q), chunked(k), chunked(v), chunked(log_f)))
    return jnp.moveaxis(o, 0, 1).reshape(bsz, t, heads, dv)


def hgrn2_layer(h, layer, w_in, w_out, lower_bounds, c_norm_w):
    bsz, t = h.shape[:2]
    proj = h @ w_in
    q, f, i, g = _split(proj, [C_HEADS * C_KEY_DIM, C_HEADS * C_KEY_DIM, C_WIDTH, C_WIDTH])
    lb_soft = jax.nn.softmax(lower_bounds.astype(jnp.float32), axis=0)
    lb = (jnp.cumsum(lb_soft, axis=0) - lb_soft[0])[layer]
    forget = lb + (1.0 - lb) * jax.nn.sigmoid(f)

    def hk(z):
        return z.reshape(bsz, t, C_HEADS, C_KEY_DIM)

    def hv(z):
        return z.reshape(bsz, t, C_HEADS, C_VAL_DIM)

    o = hgrn2_chunkwise(hk(jax.nn.silu(q)), hk(1.0 - forget), hv(i), hk(jnp.log(forget)))
    o = o * lax.rsqrt(jnp.mean(o * o, axis=-1, keepdims=True) + NORM_EPS) * c_norm_w
    o = o * jax.nn.silu(hv(g))
    return o.reshape(bsz, t, C_WIDTH) @ w_out


def setup_inputs(seed: int = 0) -> dict:
    key = jax.random.key(seed)
    ks = jax.random.split(key, 23)

    def nrm(k, shape, scale):
        return jax.random.normal(k, shape, jnp.float32) * scale

    def gain(k, n):
        return 1.0 + nrm(k, (n,), 0.05)

    return {
        'x': nrm(ks[0], (BATCH, SEQ, D_MODEL), 1.0),
        'rel_bias_table': nrm(ks[1], (REL_BUCKETS, B_HEADS), 0.5),
        'lower_bounds': nrm(ks[2], (DEPTH, C_HEADS * C_KEY_DIM), 0.1),
        'l0_pre_norm': gain(ks[3], D_MODEL),
        'l0_post_norm': gain(ks[4], D_MODEL),
        'l0_w_in': nrm(ks[5], (D_MODEL, A_COLS + B_COLS), D_MODEL ** -0.5),
        'l0_w_out': nrm(ks[6], (A_WIDTH + B_WIDTH, D_MODEL), (A_WIDTH + B_WIDTH) ** -0.5),
        'a_mu': jax.random.uniform(ks[7], (A_COLS,), jnp.float32),
        'a_w0': jax.random.uniform(ks[8], (A_WIDTH,), jnp.float32, minval=-6.0, maxval=1.0),
        'a_w2': nrm(ks[9], (A_DECAY_LORA, A_WIDTH), 0.1),
        'a_a0': nrm(ks[10], (A_WIDTH,), 0.5),
        'a_a2': nrm(ks[11], (A_ICLR_LORA, A_WIDTH), A_ICLR_LORA ** -0.5),
        'a_k_k': 0.85 + nrm(ks[12], (A_WIDTH,), 0.05),
        'a_k_a': gain(ks[13], A_WIDTH),
        'a_r_k': nrm(ks[14], (A_HEADS, A_HEAD_DIM), 0.1),
        'a_ln_w': gain(ks[15], A_WIDTH),
        'a_ln_b': nrm(ks[16], (A_WIDTH,), 0.01),
        'b_sinks': nrm(ks[17], (B_HEADS,), 1.0),
        'l1_pre_norm': gain(ks[18], D_MODEL),
        'l1_post_norm': gain(ks[19], D_MODEL),
        'l1_w_in': nrm(ks[20], (D_MODEL, C_COLS), D_MODEL ** -0.5),
        'l1_w_out': nrm(ks[21], (C_WIDTH, D_MODEL), C_WIDTH ** -0.5),
        'c_norm_w': gain(ks[22], C_VAL_DIM),
    }


def reference(x, rel_bias_table, lower_bounds, l0_pre_norm, l0_post_norm, l0_w_in, l0_w_out,
              a_mu, a_w0, a_w2, a_a0, a_a2, a_k_k, a_k_a, a_r_k, a_ln_w, a_ln_b, b_sinks,
              l1_pre_norm, l1_post_norm, l1_w_in, l1_w_out, c_norm_w):
    for layer in range(DEPTH):
        if layer % 2 == 0:
            h = rms_norm(x, l0_pre_norm)
            y = rwkv_swa_layer(h, l0_w_in, l0_w_out, a_mu, a_w0, a_w2, a_a0, a_a2, a_k_k, a_k_a,
                               a_r_k, a_ln_w, a_ln_b, b_sinks, rel_bias_table)
            x = x + rms_norm(y, l0_post_norm).astype(x.dtype)
        else:
            h = rms_norm(x, l1_pre_norm)
            y = hgrn2_layer(h, layer, l1_w_in, l1_w_out, lower_bounds, c_norm_w)
            x = x + rms_norm(y, l1_post_norm).astype(x.dtype)
    return x
```

```python
import functools
import math

import jax
import jax.numpy as jnp
import numpy as np
from jax import lax
from jax.experimental import pallas as pl
from jax.experimental.pallas import tpu as pltpu

D_MODEL = 1024
A_HEAD_DIM = 64
A_WIDTH = 512
A_LORA = 32
A_GN_EPS = 64e-5
B_HEAD_DIM = 64
B_WIDTH = 512
B_HEADS = 8
B_KV_WIDTH = 128
B_GROUP = 4
WINDOW = 128
BLOCK = 128
REL_BUCKETS = 32
REL_MAX_DIST = 128
C_HEADS = 8
C_DIM = 128
C_WIDTH = 1024
CHUNK = 64
NORM_EPS = 1e-6
NEG_INF = -1e30

LANES = 128
VMEM_LIMIT = 48 * 1024 * 1024

F32 = jnp.float32
BF16 = jnp.bfloat16


def _bf(x):
    return x.astype(BF16)


def _mm(a, b):
    return jnp.dot(_bf(a), _bf(b), preferred_element_type=F32)


def _mm_nt(a, b):
    return lax.dot_general(_bf(a), _bf(b), (((1,), (1,)), ((), ())), preferred_element_type=F32)


def _split(x, parts):
    out = []
    rem = x
    for i in range(parts):
        hi = _bf(rem)
        out.append(hi)
        if i + 1 < parts:
            rem = rem - hi.astype(F32)
    return out


def _mm_exact_lhs(a_bf, x, parts):
    n = x.shape[1]
    xs = jnp.concatenate(_split(x, parts), axis=1)
    y = jnp.dot(a_bf, xs, preferred_element_type=F32)
    acc = y[:, 0:n]
    for i in range(1, parts):
        acc = acc + y[:, i * n:(i + 1) * n]
    return acc


def _mm_exact_rhs(x, b_bf, parts):
    m = x.shape[0]
    xs = jnp.concatenate(_split(x, parts), axis=0)
    y = jnp.dot(xs, b_bf, preferred_element_type=F32)
    acc = y[0:m]
    for i in range(1, parts):
        acc = acc + y[i * m:(i + 1) * m]
    return acc


def _mm3(a, b):
    a_hi, a_lo = _split(a, 2)
    b_hi, b_lo = _split(b, 2)
    return (jnp.dot(a_hi, b_hi, preferred_element_type=F32)
            + jnp.dot(a_hi, b_lo, preferred_element_type=F32)
            + jnp.dot(a_lo, b_hi, preferred_element_type=F32))


def _iota(shape, dim):
    return lax.broadcasted_iota(jnp.int32, shape, dim)


def _silu(x):
    return x * jax.nn.sigmoid(x)


def _softplus(z):
    return jnp.maximum(z, 0.0) + jnp.log1p(jnp.exp(-jnp.abs(z)))


def _tril_ones(n):
    return jnp.where(_iota((n, n), 1) <= _iota((n, n), 0), 1.0, 0.0).astype(BF16)


def _cparams(sem):
    return pltpu.CompilerParams(dimension_semantics=sem, vmem_limit_bytes=VMEM_LIMIT)


def _l0_in_kernel(x_ref, nw_ref, w_rkv, w_lora, w_ga, w_qb, w_kvb, w_gb, mu_rkv, mu_lora, mu_ga,
                  o_rkv, o_lora, o_ga, o_qb, o_kvb, o_gb, c_rkv, c_lora, c_ga):
    t = pl.program_id(1)

    @pl.when(t == 0)
    def _():
        c_rkv[...] = jnp.zeros_like(c_rkv)
        c_lora[...] = jnp.zeros_like(c_lora)
        c_ga[...] = jnp.zeros_like(c_ga)

    x = x_ref[0]
    h = x * lax.rsqrt(jnp.mean(x * x, axis=-1, keepdims=True) + NORM_EPS) * nw_ref[...]
    hb = _bf(h)
    tm = x.shape[0]

    def shifted(w, mu, carry, out):
        p = jnp.dot(hb, w[...], preferred_element_type=F32)
        prev = pltpu.roll(p, 1, axis=0)
        row = _iota(p.shape, 0)
        prev = jnp.where(row == 0, carry[0:1, :], prev)
        carry[0:1, :] = p[tm - 1:tm, :]
        out[0] = p + mu[...] * (prev - p)

    shifted(w_rkv, mu_rkv, c_rkv, o_rkv)
    shifted(w_lora, mu_lora, c_lora, o_lora)
    shifted(w_ga, mu_ga, c_ga, o_ga)
    o_qb[0] = jnp.dot(hb, w_qb[...], preferred_element_type=F32)
    o_kvb[0] = jnp.dot(hb, w_kvb[...], preferred_element_type=F32)
    o_gb[0] = jnp.dot(hb, w_gb[...], preferred_element_type=F32)


def _l0_in(x, nw, ws, mus, tm):
    bsz, t, d = x.shape
    widths = [w.shape[1] for w in ws]
    full = lambda a: pl.BlockSpec(a.shape, lambda b, i: (0,) * a.ndim)
    in_specs = [pl.BlockSpec((1, tm, d), lambda b, i: (b, i, 0)), full(nw)]
    in_specs += [full(w) for w in ws] + [full(m) for m in mus]
    out_specs = [pl.BlockSpec((1, tm, n), lambda b, i: (b, i, 0)) for n in widths]
    out_shape = [jax.ShapeDtypeStruct((bsz, t, n), F32) for n in widths]
    scratch = [pltpu.VMEM((8, widths[0]), F32), pltpu.VMEM((8, widths[1]), F32), pltpu.VMEM((8, widths[2]), F32)]
    return pl.pallas_call(
        _l0_in_kernel, grid=(bsz, t // tm), in_specs=in_specs, out_specs=out_specs, out_shape=out_shape,
        scratch_shapes=scratch, compiler_params=_cparams(("parallel", "arbitrary")), name="l0_in",
    )(x, nw, *ws, *mus)


def _bd(x):
    h0 = _iota(x.shape, 1) < A_HEAD_DIM
    return jnp.concatenate([jnp.where(h0, x, 0.0), jnp.where(h0, 0.0, x)], axis=0)


def _unit_lower_inverse(n_packed):
    c = n_packed.shape[0]
    row = _iota(n_packed.shape, 0)
    col = _iota(n_packed.shape, 1) & (A_HEAD_DIM - 1)
    t = jnp.where(col == row, 1.0, 0.0) + jnp.where((col == row - 1) & ((row & 1) == 1), n_packed, 0.0)
    bs = 2
    while bs < c:
        off = jnp.where(((row // (2 * bs)) == (col // (2 * bs))) & (((row // bs) & 1) == 1) & (((col // bs) & 1) == 0),
                        n_packed, 0.0)
        x = _mm(off, _bd(t))
        t = t + _mm(t, _bd(x))
        bs *= 2
    return t


def _rwkv_kernel(rkv_ref, lora_ref, ga_ref, wcomb_ref, vec_ref, o_ref, s_ref):
    ti = pl.program_id(1)

    @pl.when(ti == 0)
    def _():
        s_ref[...] = jnp.zeros_like(s_ref)

    tc = rkv_ref.shape[1]
    c = CHUNK
    n_pairs = A_WIDTH // LANES
    w0 = vec_ref[0:1, :]
    a0 = vec_ref[1:2, :]
    k_k = vec_ref[2:3, :]
    k_a = vec_ref[3:4, :]
    r_k = vec_ref[4:5, :]
    ln_w = vec_ref[5:6, :]
    ln_b = vec_ref[6:7, :]

    tril = _tril_ones(c)
    li = _iota((LANES, LANES), 0) // A_HEAD_DIM
    lj = _iota((LANES, LANES), 1) // A_HEAD_DIM
    head_ones = jnp.where(li == lj, 1.0, 0.0).astype(BF16)
    bd_mask = li == lj

    prow = _iota((c, LANES), 0)
    pcol = _iota((c, LANES), 1) & (A_HEAD_DIM - 1)
    strict = pcol < prow
    incl = pcol <= prow
    head0 = _iota((c, LANES), 1) < A_HEAD_DIM
    lane64 = _iota((c, 2 * A_LORA), 1)

    def headsum(x):
        outs = []
        for p in range(n_pairs):
            outs.append(_mm_exact_rhs(x[:, p * LANES:(p + 1) * LANES], head_ones, 2))
        return jnp.concatenate(outs, axis=1)

    def chunk_body(ci, carry):
        r0 = pl.multiple_of(ci * c, c)
        rows = pl.ds(r0, c)
        r = rkv_ref[0, rows, 0:A_WIDTH]
        k = rkv_ref[0, rows, A_WIDTH:2 * A_WIDTH]
        v = rkv_ref[0, rows, 2 * A_WIDTH:3 * A_WIDTH]
        lo = lora_ref[0, rows, :]
        gate = ga_ref[0, rows, :]

        z = jnp.where(lane64 < A_LORA, jnp.tanh(lo), lo)
        wa = _mm3(z, wcomb_ref[...])
        w = -_softplus(-(w0 + wa[:, 0:A_WIDTH])) - 0.5
        logw = -jnp.exp(w)
        a = jax.nn.sigmoid(a0 + wa[:, A_WIDTH:2 * A_WIDTH])
        kk = k * k_k
        kkn = kk / jnp.maximum(jnp.sqrt(headsum(kk * kk)), 1e-12)
        k2 = k * (1.0 + (a - 1.0) * k_a)
        ah = -kkn
        bb = kkn * a

        g = _mm_exact_lhs(tril, logw, 3)
        ge = g - logw
        g_last = g[c - 1:c, :]
        e_neg = jnp.exp(-g)
        e_tail = jnp.exp(g_last - g)
        a_dec = ah * jnp.exp(ge)
        r_dec = r * jnp.exp(g)
        b_inv = bb * e_neg
        k_inv = k2 * e_neg
        b_tail = bb * e_tail
        k_tail = k2 * e_tail
        decay_c = jnp.exp(g_last)

        ys = []
        for p in range(n_pairs):
            sl = slice(p * LANES, (p + 1) * LANES)
            lhs = jnp.concatenate([a_dec[:, sl], r_dec[:, sl]], axis=0)
            bp, kp, vp = b_inv[:, sl], k_inv[:, sl], v[:, sl]
            rhs = jnp.concatenate([jnp.where(head0, bp, 0.0), jnp.where(head0, 0.0, bp),
                                   jnp.where(head0, kp, 0.0), jnp.where(head0, 0.0, kp)], axis=0)
            gm = _mm_nt(lhs, rhs)
            n_ab = jnp.where(strict, gm[0:c, 0:LANES], 0.0)
            a_ak = jnp.where(strict, gm[0:c, LANES:2 * LANES], 0.0)
            a_rb = jnp.where(incl, gm[c:2 * c, 0:LANES], 0.0)
            a_rk = jnp.where(incl, gm[c:2 * c, LANES:2 * LANES], 0.0)
            tinv = _unit_lower_inverse(n_ab)

            st = s_ref[p]
            from_state = _mm_nt(lhs, st)
            from_v = _mm(jnp.concatenate([a_ak, a_rk], axis=0), _bd(vp))
            u = _mm(tinv, _bd(from_state[0:c] + from_v[0:c]))
            y = from_state[c:2 * c] + from_v[c:2 * c] + _mm(a_rb, _bd(u))
            ys.append(y)

            zt = jnp.concatenate([u, vp], axis=0)
            wg = jnp.concatenate([b_tail[:, sl], k_tail[:, sl]], axis=0)
            upd = _mm(zt.T, wg)
            s_ref[p] = st * decay_c[:, sl] + jnp.where(bd_mask, upd, 0.0)

        y = jnp.concatenate(ys, axis=1)
        inv_n = 1.0 / A_HEAD_DIM
        mean = headsum(y) * inv_n
        d = y - mean
        var = headsum(d * d) * inv_n
        yn = d * lax.rsqrt(var + A_GN_EPS) * ln_w + ln_b
        yn = yn + headsum(r * k2 * r_k) * v
        o_ref[0, rows, :] = yn * _silu(gate)
        return carry

    lax.fori_loop(0, tc // c, chunk_body, 0)


def _rwkv(rkv, lora, ga, wcomb, vecs, tc):
    bsz, t, _ = rkv.shape
    full = lambda a: pl.BlockSpec(a.shape, lambda b, i: (0,) * a.ndim)
    tok = lambda n: pl.BlockSpec((1, tc, n), lambda b, i: (b, i, 0))
    return pl.pallas_call(
        _rwkv_kernel, grid=(bsz, t // tc),
        in_specs=[tok(3 * A_WIDTH), tok(2 * A_LORA), tok(A_WIDTH), full(wcomb), full(vecs)],
        out_specs=tok(A_WIDTH), out_shape=jax.ShapeDtypeStruct((bsz, t, A_WIDTH), F32),
        scratch_shapes=[pltpu.VMEM((A_WIDTH // LANES, LANES, LANES), F32)],
        compiler_params=_cparams(("parallel", "arbitrary")), name="rwkv7",
    )(rkv, lora, ga, wcomb, vecs)


def _swa_kernel(sink_ref, q_ref, kv_ref, kvp_ref, g_ref, bias_ref, o_ref):
    n = pl.program_id(1)
    q = q_ref[0]
    kv = jnp.concatenate([kvp_ref[0], kv_ref[0]], axis=0)
    kc = kv[:, 0:B_KV_WIDTH]
    vc = kv[:, B_KV_WIDTH:2 * B_KV_WIDTH]
    kr = pltpu.roll(kc, B_HEAD_DIM, axis=1)
    vr = pltpu.roll(vc, B_HEAD_DIM, axis=1)
    h0 = _iota(kc.shape, 1) < B_HEAD_DIM

    def ext(straight, rolled, kvh, e):
        src = straight if kvh == e else rolled
        return jnp.where(h0, src, 0.0) if e == 0 else jnp.where(h0, 0.0, src)

    qi = _iota((BLOCK, 2 * BLOCK), 0)
    kj = _iota((BLOCK, 2 * BLOCK), 1)
    dist = qi + BLOCK - kj
    valid = (dist >= 0) & (dist < WINDOW) & ((n > 0) | (kj >= BLOCK))
    scale = B_HEAD_DIM ** -0.5

    for p in range(B_WIDTH // LANES):
        kvh = (2 * p) // B_GROUP
        qp = q[:, p * LANES:(p + 1) * LANES]
        acc = jnp.zeros((BLOCK, LANES), F32)
        for e in range(2):
            hq = 2 * p + e
            s = _mm_nt(qp, ext(kc, kr, kvh, e)) * scale
            s = jnp.where(valid, s + bias_ref[hq], NEG_INF)
            sink = sink_ref[hq]
            m = jnp.maximum(jnp.max(s, axis=-1, keepdims=True), sink)
            pe = jnp.exp(s - m)
            denom = jnp.sum(pe, axis=-1, keepdims=True) + jnp.exp(sink - m)
            acc = acc + _mm(pe / denom, ext(vc, vr, kvh, e))
        o_ref[0, :, p * LANES:(p + 1) * LANES] = acc * _silu(g_ref[0, :, p * LANES:(p + 1) * LANES])


def _swa(sinks, qb, kvb, gb, bias):
    bsz, t, _ = qb.shape
    nb = t // BLOCK
    return pl.pallas_call(
        _swa_kernel, grid=(bsz, nb),
        in_specs=[pl.BlockSpec(memory_space=pltpu.SMEM),
                  pl.BlockSpec((1, BLOCK, B_WIDTH), lambda b, i: (b, i, 0)),
                  pl.BlockSpec((1, BLOCK, 2 * B_KV_WIDTH), lambda b, i: (b, i, 0)),
                  pl.BlockSpec((1, BLOCK, 2 * B_KV_WIDTH), lambda b, i: (b, jnp.maximum(i - 1, 0), 0)),
                  pl.BlockSpec((1, BLOCK, B_WIDTH), lambda b, i: (b, i, 0)),
                  pl.BlockSpec(bias.shape, lambda b, i: (0, 0, 0))],
        out_specs=pl.BlockSpec((1, BLOCK, B_WIDTH), lambda b, i: (b, i, 0)),
        out_shape=jax.ShapeDtypeStruct((bsz, t, B_WIDTH), F32),
        compiler_params=_cparams(("parallel", "arbitrary")), name="swa",
    )(sinks, qb, kvb, kvb, gb, bias)


def _out_kernel(*refs, n_parts):
    y_refs = refs[:n_parts]
    w_refs = refs[n_parts:2 * n_parts]
    nw_ref, x_ref, o_ref = refs[2 * n_parts:]
    y = None
    for yr, wr in zip(y_refs, w_refs):
        part = jnp.dot(_bf(yr[0]), wr[...], preferred_element_type=F32)
        y = part if y is None else y + part
    yn = y * lax.rsqrt(jnp.mean(y * y, axis=-1, keepdims=True) + NORM_EPS) * nw_ref[...]
    o_ref[0] = x_ref[0] + yn


def _out_proj(ys, ws, nw, x, tm):
    bsz, t, d = x.shape
    full = lambda a: pl.BlockSpec(a.shape, lambda b, i: (0,) * a.ndim)
    tok = lambda n: pl.BlockSpec((1, tm, n), lambda b, i: (b, i, 0))
    return pl.pallas_call(
        functools.partial(_out_kernel, n_parts=len(ys)), grid=(bsz, t // tm),
        in_specs=[tok(y.shape[2]) for y in ys] + [full(w) for w in ws] + [full(nw), tok(d)],
        out_specs=tok(d), out_shape=jax.ShapeDtypeStruct((bsz, t, d), F32),
        compiler_params=_cparams(("parallel", "parallel")), name="out_proj",
    )(*ys, *ws, nw, x)


def _l1_in_kernel(x_ref, nw_ref, w_ref, o_ref):
    x = x_ref[0]
    h = x * lax.rsqrt(jnp.mean(x * x, axis=-1, keepdims=True) + NORM_EPS) * nw_ref[...]
    o_ref[0] = jnp.dot(_bf(h), w_ref[...], preferred_element_type=F32)


def _l1_in(x, nw, w, tm):
    bsz, t, d = x.shape
    n = w.shape[1]
    return pl.pallas_call(
        _l1_in_kernel, grid=(bsz, t // tm),
        in_specs=[pl.BlockSpec((1, tm, d), lambda b, i: (b, i, 0)),
                  pl.BlockSpec(nw.shape, lambda b, i: (0, 0)),
                  pl.BlockSpec(w.shape, lambda b, i: (0, 0))],
        out_specs=pl.BlockSpec((1, tm, n), lambda b, i: (b, i, 0)),
        out_shape=jax.ShapeDtypeStruct((bsz, t, n), F32),
        compiler_params=_cparams(("parallel", "parallel")), name="l1_in",
    )(x, nw, w)


def _hgrn_kernel(q_ref, f_ref, i_ref, g_ref, lb_ref, cw_ref, o_ref, s_ref):
    ti = pl.program_id(2)

    @pl.when(ti == 0)
    def _():
        s_ref[...] = jnp.zeros_like(s_ref)

    tc = q_ref.shape[1]
    c = CHUNK
    l0 = lb_ref[0:1, :]
    l1 = lb_ref[1:2, :]
    m = jnp.maximum(l0, l1)
    e0 = jnp.exp(l0 - m)
    e1 = jnp.exp(l1 - m)
    s0 = e0 / (e0 + e1)
    s1 = e1 / (e0 + e1)
    lb = (s0 + s1) - s0
    cw = cw_ref[...]
    tril = _tril_ones(c)
    causal = _iota((c, c), 1) <= _iota((c, c), 0)

    def chunk_body(ci, carry):
        r0 = pl.multiple_of(ci * c, c)
        rows = pl.ds(r0, c)
        qq = q_ref[0, rows, :]
        ff = f_ref[0, rows, :]
        vv = i_ref[0, rows, :]
        gg = g_ref[0, rows, :]
        forget = lb + (1.0 - lb) * jax.nn.sigmoid(ff)
        gc = _mm_exact_lhs(tril, jnp.log(forget), 3)
        g_last = gc[c - 1:c, :]
        kf = 1.0 - forget
        q_dec = _silu(qq) * jnp.exp(gc)
        k_inv = kf * jnp.exp(-gc)
        k_tail = kf * jnp.exp(g_last - gc)
        st = s_ref[...]
        sc = jnp.where(causal, _mm_nt(q_dec, k_inv), 0.0)
        o = _mm(sc, vv) + _mm_nt(q_dec, st)
        s_ref[...] = st * jnp.exp(g_last) + _mm(vv.T, k_tail)
        on = o * lax.rsqrt(jnp.mean(o * o, axis=-1, keepdims=True) + NORM_EPS) * cw
        o_ref[0, rows, :] = on * _silu(gg)
        return carry

    lax.fori_loop(0, tc // c, chunk_body, 0)


def _hgrn(proj, lbs, cw, tc):
    bsz, t, _ = proj.shape
    col = lambda off: pl.BlockSpec((1, tc, C_DIM), lambda b, h, i: (b, i, off + h))
    return pl.pallas_call(
        _hgrn_kernel, grid=(bsz, C_HEADS, t // tc),
        in_specs=[col(0), col(C_HEADS), col(2 * C_HEADS), col(3 * C_HEADS),
                  pl.BlockSpec((lbs.shape[0], C_DIM), lambda b, h, i: (0, h)),
                  pl.BlockSpec(cw.shape, lambda b, h, i: (0, 0))],
        out_specs=pl.BlockSpec((1, tc, C_DIM), lambda b, h, i: (b, i, h)),
        out_shape=jax.ShapeDtypeStruct((bsz, t, C_WIDTH), F32),
        scratch_shapes=[pltpu.VMEM((C_DIM, C_DIM), F32)],
        compiler_params=_cparams(("parallel", "parallel", "arbitrary")), name="hgrn2",
    )(proj, proj, proj, proj, lbs, cw)


def _t5_bucket(dist):
    max_exact = REL_BUCKETS // 2
    d = jnp.maximum(dist, 0)
    large = max_exact + (jnp.log(jnp.maximum(d, 1).astype(F32) / max_exact)
                         / math.log(REL_MAX_DIST / max_exact)
                         * (REL_BUCKETS - max_exact)).astype(jnp.int32)
    large = jnp.minimum(large, REL_BUCKETS - 1)
    return jnp.where(d < max_exact, d, large)


def _tile(t, want):
    return want if t % want == 0 else BLOCK


def kernel(x, rel_bias_table, lower_bounds, l0_pre_norm, l0_post_norm, l0_w_in, l0_w_out, a_mu, a_w0, a_w2, a_a0, a_a2, a_k_k, a_k_a, a_r_k, a_ln_w, a_ln_b, b_sinks, l1_pre_norm, l1_post_norm, l1_w_in, l1_w_out, c_norm_w):
    bsz, t, d = x.shape
    assert d == D_MODEL and t % BLOCK == 0
    row = lambda v: v.reshape(1, -1).astype(F32)

    c_rkv, c_lora, c_ga = 3 * A_WIDTH, 3 * A_WIDTH + 2 * A_LORA, 4 * A_WIDTH + 2 * A_LORA
    c_q, c_kv, c_gb = c_ga + B_WIDTH, c_ga + B_WIDTH + 2 * B_KV_WIDTH, c_ga + 2 * B_WIDTH + 2 * B_KV_WIDTH
    wb = _bf(l0_w_in)
    ws = [wb[:, 0:c_rkv], wb[:, c_rkv:c_lora], wb[:, c_lora:c_ga], wb[:, c_ga:c_q], wb[:, c_q:c_kv], wb[:, c_kv:c_gb]]
    mus = [row(a_mu[0:c_rkv]), row(a_mu[c_rkv:c_lora]), row(a_mu[c_lora:c_ga])]
    zeros = jnp.zeros((A_LORA, A_WIDTH), F32)
    wcomb = jnp.concatenate([jnp.concatenate([a_w2, zeros], axis=1), jnp.concatenate([zeros, a_a2], axis=1)], axis=0)
    vecs = jnp.stack([a_w0, a_a0, a_k_k, a_k_a, a_r_k.reshape(-1), a_ln_w, a_ln_b, jnp.zeros_like(a_w0)]).astype(F32)
    qi = jnp.arange(BLOCK)[:, None]
    kj = jnp.arange(2 * BLOCK)[None, :]
    bias = jnp.transpose(rel_bias_table[_t5_bucket(qi + BLOCK - kj)], (2, 0, 1)).astype(F32)
    w_out0 = _bf(l0_w_out)
    w_in1 = _bf(l1_w_in)
    w_out1 = _bf(l1_w_out)

    tm = _tile(t, 256)
    rkv, lora, ga, qb, kvb, gb = _l0_in(x, row(l0_pre_norm), ws, mus, tm)
    ya = _rwkv(rkv, lora, ga, wcomb, vecs, _tile(t, 256))
    yb = _swa(b_sinks.astype(F32), qb, kvb, gb, bias)
    x1 = _out_proj([ya, yb], [w_out0[0:A_WIDTH], w_out0[A_WIDTH:]], row(l0_post_norm), x, tm)
    proj = _l1_in(x1, row(l1_pre_norm), w_in1, tm)
    o = _hgrn(proj, lower_bounds.astype(F32), row(c_norm_w), _tile(t, 1024))
    return _out_proj([o], [w_out1], row(l1_post_norm), x1, tm)
```

```python
import functools
import math

import jax
import jax.numpy as jnp
from jax import lax
from jax.experimental import pallas as pl
from jax.experimental.pallas import tpu as pltpu

D_MODEL = 1024
A_HEAD_DIM = 64
A_WIDTH = 512
A_LORA = 32
A_GN_EPS = 64e-5
B_HEAD_DIM = 64
B_WIDTH = 512
B_HEADS = 8
B_KV_WIDTH = 128
B_GROUP = 4
WINDOW = 128
BLOCK = 128
REL_BUCKETS = 32
REL_MAX_DIST = 128
C_HEADS = 8
C_DIM = 128
C_WIDTH = 1024
CHUNK = 64
NORM_EPS = 1e-6
NEG_INF = -1e30

LANES = 128
MXU_DEPTH = 256
N_PAIRS = A_WIDTH // LANES
A_PAR = 4
C_PAR = 2
VMEM_LIMIT = 48 * 1024 * 1024

F32 = jnp.float32
BF16 = jnp.bfloat16


def _bf(x):
    return x.astype(BF16)


def _mm(a, b):
    return jnp.dot(_bf(a), _bf(b), preferred_element_type=F32)


def _mm_nt(a, b):
    return lax.dot_general(_bf(a), _bf(b), (((1,), (1,)), ((), ())), preferred_element_type=F32)


def _split(x, parts):
    out = []
    rem = x
    for i in range(parts):
        hi = _bf(rem)
        out.append(hi)
        if i + 1 < parts:
            rem = rem - hi.astype(F32)
    return out


def _mm_exact_lhs(a_bf, x, parts):
    n = x.shape[1]
    xs = jnp.concatenate(_split(x, parts), axis=1)
    y = jnp.dot(a_bf, xs, preferred_element_type=F32)
    acc = y[:, 0:n]
    for i in range(1, parts):
        acc = acc + y[:, i * n:(i + 1) * n]
    return acc


def _mm_exact_rhs(x, b_stacked, parts):
    return jnp.dot(jnp.concatenate(_split(x, parts), axis=1), b_stacked, preferred_element_type=F32)


def _mm3(a, b):
    a_hi, a_lo = _split(a, 2)
    b_hi, b_lo = _split(b, 2)
    return jnp.dot(jnp.concatenate([a_hi, a_hi, a_lo], axis=1), jnp.concatenate([b_hi, b_lo, b_hi], axis=0),
                   preferred_element_type=F32)


def _iota(shape, dim):
    return lax.broadcasted_iota(jnp.int32, shape, dim)


def _sigmoid(x):
    return 0.5 * (1.0 + jnp.tanh(0.5 * x))


def _silu(x):
    return x * _sigmoid(x)


def _softplus(z):
    return jnp.maximum(z, 0.0) + jnp.log(1.0 + jnp.exp(-jnp.abs(z)))


def _chunk_cumsum(x, c):
    t = x.shape[0]
    blk = min(t, MXU_DEPTH)
    i = _iota((blk, blk), 0)
    j = _iota((blk, blk), 1)
    cum = jnp.where(((i // c) == (j // c)) & (j <= i), 1.0, 0.0).astype(BF16)
    return jnp.concatenate([_mm_exact_lhs(cum, x[b * blk:(b + 1) * blk], 3) for b in range(t // blk)], axis=0)


def _cparams(sem):
    return pltpu.CompilerParams(dimension_semantics=sem, vmem_limit_bytes=VMEM_LIMIT)


def _l0_in_kernel(x_ref, nw_ref, w_rkv, w_lora, w_ga, w_qb, w_kvb, w_gb, mu_rkv, mu_lora, mu_ga,
                  o_rkv, o_lora, o_ga, o_qb, o_kvb, o_gb, c_rkv, c_lora, c_ga):
    t = pl.program_id(1)

    @pl.when(t == 0)
    def _():
        c_rkv[...] = jnp.zeros_like(c_rkv)
        c_lora[...] = jnp.zeros_like(c_lora)
        c_ga[...] = jnp.zeros_like(c_ga)

    x = x_ref[0]
    h = x * lax.rsqrt(jnp.mean(x * x, axis=-1, keepdims=True) + NORM_EPS) * nw_ref[...]
    hb = _bf(h)
    tm = x.shape[0]

    def shifted(w, mu, carry, out):
        p = jnp.dot(hb, w[...], preferred_element_type=F32)
        prev = pltpu.roll(p, 1, axis=0)
        row = _iota(p.shape, 0)
        prev = jnp.where(row == 0, carry[0:1, :], prev)
        carry[0:1, :] = p[tm - 1:tm, :]
        out[0] = p + mu[...] * (prev - p)

    shifted(w_rkv, mu_rkv, c_rkv, o_rkv)
    shifted(w_lora, mu_lora, c_lora, o_lora)
    shifted(w_ga, mu_ga, c_ga, o_ga)
    o_qb[0] = jnp.dot(hb, w_qb[...], preferred_element_type=F32)
    o_kvb[0] = jnp.dot(hb, w_kvb[...], preferred_element_type=F32)
    o_gb[0] = jnp.dot(hb, w_gb[...], preferred_element_type=F32)


def _l0_in(x, nw, ws, mus, tm):
    bsz, t, d = x.shape
    widths = [w.shape[1] for w in ws]
    full = lambda a: pl.BlockSpec(a.shape, lambda b, i: (0,) * a.ndim)
    in_specs = [pl.BlockSpec((1, tm, d), lambda b, i: (b, i, 0)), full(nw)]
    in_specs += [full(w) for w in ws] + [full(m) for m in mus]
    out_specs = [pl.BlockSpec((1, tm, n), lambda b, i: (b, i, 0)) for n in widths]
    out_shape = [jax.ShapeDtypeStruct((bsz, t, n), F32) for n in widths]
    scratch = [pltpu.VMEM((8, widths[0]), F32), pltpu.VMEM((8, widths[1]), F32), pltpu.VMEM((8, widths[2]), F32)]
    return pl.pallas_call(
        _l0_in_kernel, grid=(bsz, t // tm), in_specs=in_specs, out_specs=out_specs, out_shape=out_shape,
        scratch_shapes=scratch, compiler_params=_cparams(("parallel", "arbitrary")), name="l0_in",
    )(x, nw, *ws, *mus)


def _bd(x):
    h0 = _iota(x.shape, 1) < A_HEAD_DIM
    return jnp.concatenate([jnp.where(h0, x, 0.0), jnp.where(h0, 0.0, x)], axis=0)


def _unit_lower_inverse(ns):
    c = ns[0].shape[0]
    row = _iota(ns[0].shape, 0)
    col = _iota(ns[0].shape, 1) & (A_HEAD_DIM - 1)
    eye = jnp.where(col == row, 1.0, 0.0)
    first = (col == row - 1) & ((row & 1) == 1)
    ts = [eye + jnp.where(first, n, 0.0) for n in ns]
    bs = 2
    while bs < c:
        sel = ((row // (2 * bs)) == (col // (2 * bs))) & (((row // bs) & 1) == 1) & (((col // bs) & 1) == 0)
        xs = [_mm(jnp.where(sel, n, 0.0), _bd(t)) for n, t in zip(ns, ts)]
        ts = [t + _mm(t, _bd(x)) for t, x in zip(ts, xs)]
        bs *= 2
    return ts


def _rwkv_kernel(rkv_ref, lora_ref, ga_ref, wcomb_ref, vec_ref, o_ref,
                 s_ref, ad_sc, rd_sc, bi_sc, ki_sc, dc_sc, q_sc, y_sc, w_sc, ut_sc, vk_sc, *, par):
    ti = pl.program_id(1)

    @pl.when(ti == 0)
    def _():
        s_ref[...] = jnp.zeros_like(s_ref)

    tc = rkv_ref.shape[1]
    c = CHUNK
    n_chunks = tc // c
    w0 = vec_ref[0:1, :]
    a0 = vec_ref[1:2, :]
    k_k = vec_ref[2:3, :]
    k_a = vec_ref[3:4, :]
    r_k = vec_ref[4:5, :]
    ln_w = vec_ref[5:6, :]
    ln_b = vec_ref[6:7, :]

    li = _iota((LANES, LANES), 0) // A_HEAD_DIM
    lj = _iota((LANES, LANES), 1) // A_HEAD_DIM
    bd_mask = li == lj
    head_ones = jnp.where(bd_mask, 1.0, 0.0).astype(BF16)
    head_ones2 = jnp.concatenate([head_ones, head_ones], axis=0)
    pairs = [slice(p * LANES, (p + 1) * LANES) for p in range(N_PAIRS)]

    def headsum(x):
        return jnp.concatenate([_mm_exact_rhs(x[:, sl], head_ones2, 2) for sl in pairs], axis=1)

    r = rkv_ref[0, :, 0:A_WIDTH]
    k = rkv_ref[0, :, A_WIDTH:2 * A_WIDTH]
    v = rkv_ref[0, :, 2 * A_WIDTH:3 * A_WIDTH]
    lo = lora_ref[0]
    z = jnp.where(_iota(lo.shape, 1) < A_LORA, jnp.tanh(lo), lo)
    wa = _mm3(z, wcomb_ref[...])
    w = -_softplus(-(w0 + wa[:, 0:A_WIDTH])) - 0.5
    logw = -jnp.exp(w)
    a = _sigmoid(a0 + wa[:, A_WIDTH:2 * A_WIDTH])
    kk = k * k_k
    kkn = kk * lax.rsqrt(jnp.maximum(headsum(kk * kk), 1e-24))
    k2 = k * (1.0 + (a - 1.0) * k_a)
    g = _chunk_cumsum(logw, c)
    e_neg = jnp.exp(-g)
    ad_sc[...] = -kkn * jnp.exp(g - logw)
    rd_sc[...] = r * jnp.exp(g)
    bi_sc[...] = kkn * a * e_neg
    ki_sc[...] = k2 * e_neg
    for j in range(n_chunks):
        dc_sc[j * 8:(j + 1) * 8, :] = jnp.broadcast_to(jnp.exp(g[(j + 1) * c - 1:(j + 1) * c, :]), (8, A_WIDTH))
    o_ref[0] = headsum(r * k2 * r_k) * v

    prow = _iota((c, LANES), 0)
    pcol = _iota((c, LANES), 1) & (A_HEAD_DIM - 1)
    strict = pcol < prow
    incl = pcol <= prow
    head0 = _iota((c, LANES), 1) < A_HEAD_DIM
    chains = [(j, p) for j in range(par) for p in range(N_PAIRS)]

    def local_body(gi, carry):
        def rows(j):
            return pl.ds(pl.multiple_of((gi * par + j) * c, c), c)

        a_dec = [ad_sc[rows(j), pairs[p]] for j, p in chains]
        r_dec = [rd_sc[rows(j), pairs[p]] for j, p in chains]
        b_inv = [bi_sc[rows(j), pairs[p]] for j, p in chains]
        k_inv = [ki_sc[rows(j), pairs[p]] for j, p in chains]
        vv = [rkv_ref[0, rows(j), 2 * A_WIDTH + p * LANES:2 * A_WIDTH + (p + 1) * LANES] for j, p in chains]
        k_tail = [ki * dc_sc[pl.ds(pl.multiple_of((gi * par + j) * 8, 8), 1), pairs[p]]
                  for ki, (j, p) in zip(k_inv, chains)]

        gms = [_mm_nt(jnp.concatenate([ad, rd], axis=0),
                      jnp.concatenate([jnp.where(head0, bp, 0.0), jnp.where(head0, 0.0, bp),
                                       jnp.where(head0, kp, 0.0), jnp.where(head0, 0.0, kp)], axis=0))
               for ad, rd, bp, kp in zip(a_dec, r_dec, b_inv, k_inv)]
        n_ab = [jnp.where(strict, gm[0:c, 0:LANES], 0.0) for gm in gms]
        a_ak = [jnp.where(strict, gm[0:c, LANES:2 * LANES], 0.0) for gm in gms]
        a_rb = [jnp.where(incl, gm[c:2 * c, 0:LANES], 0.0) for gm in gms]
        a_rk = [jnp.where(incl, gm[c:2 * c, LANES:2 * LANES], 0.0) for gm in gms]
        tinv = _unit_lower_inverse(n_ab)
        from_v = [_mm(jnp.concatenate([ak, rk], axis=0), _bd(x)) for ak, rk, x in zip(a_ak, a_rk, vv)]
        tw = [_mm(t, jnp.concatenate([_bd(ad), _bd(fv[0:c])], axis=1))
              for t, fv, ad in zip(tinv, from_v, a_dec)]
        ab = [_mm(rb, jnp.concatenate([_bd(x[:, 0:LANES]), _bd(x[:, LANES:2 * LANES])], axis=1))
              for rb, x in zip(a_rb, tw)]
        vk = [_mm(x.T, kt) for x, kt in zip(vv, k_tail)]
        for i, (j, p) in enumerate(chains):
            slot = (gi * par + j) * N_PAIRS + p
            w_sc[rows(j), pairs[p]] = tw[i][:, 0:LANES]
            q_sc[rows(j), pairs[p]] = r_dec[i] + ab[i][:, 0:LANES]
            y_sc[rows(j), pairs[p]] = ab[i][:, LANES:2 * LANES] + from_v[i][c:2 * c]
            ut_sc[slot] = tw[i][:, LANES:2 * LANES].T
            vk_sc[slot] = jnp.where(bd_mask, vk[i], 0.0)
        return carry

    lax.fori_loop(0, n_chunks // par, local_body, 0)

    def state_body(ci, carry):
        r0 = pl.multiple_of(ci * c, c)
        rows = pl.ds(r0, c)
        dc = dc_sc[pl.ds(pl.multiple_of(ci * 8, 8), 1), :]
        sts = [s_ref[p] for p in range(N_PAIRS)]
        uts = [_mm_nt(st, w_sc[rows, sl]) + ut_sc[ci * N_PAIRS + p] for p, (st, sl) in enumerate(zip(sts, pairs))]
        ys = [_mm_nt(q_sc[rows, sl], st) for st, sl in zip(sts, pairs)]
        upd = [_mm(ut, bi_sc[rows, sl] * dc[:, sl]) for ut, sl in zip(uts, pairs)]
        for p, sl in enumerate(pairs):
            s_ref[p] = sts[p] * dc[:, sl] + jnp.where(bd_mask, upd[p], 0.0) + vk_sc[ci * N_PAIRS + p]
            y_sc[rows, sl] = y_sc[rows, sl] + ys[p]
        return carry

    lax.fori_loop(0, n_chunks, state_body, 0)

    y = y_sc[...]
    inv_n = 1.0 / A_HEAD_DIM
    mean = headsum(y) * inv_n
    d = y - mean
    var = headsum(d * d) * inv_n
    yn = d * lax.rsqrt(var + A_GN_EPS) * ln_w + ln_b
    o_ref[0] = (yn + o_ref[0]) * _silu(ga_ref[0])


def _rwkv(rkv, lora, ga, wcomb, vecs, tc):
    bsz, t, _ = rkv.shape
    n_chunks = tc // CHUNK
    par = math.gcd(A_PAR, n_chunks)
    full = lambda a: pl.BlockSpec(a.shape, lambda b, i: (0,) * a.ndim)
    tok = lambda n: pl.BlockSpec((1, tc, n), lambda b, i: (b, i, 0))
    tile = pltpu.VMEM((tc, A_WIDTH), F32)
    scratch = [pltpu.VMEM((N_PAIRS, LANES, LANES), F32),
               tile, tile, tile, tile,
               pltpu.VMEM((n_chunks * 8, A_WIDTH), F32),
               tile, tile, tile,
               pltpu.VMEM((n_chunks * N_PAIRS, LANES, CHUNK), F32),
               pltpu.VMEM((n_chunks * N_PAIRS, LANES, LANES), F32)]
    return pl.pallas_call(
        functools.partial(_rwkv_kernel, par=par), grid=(bsz, t // tc),
        in_specs=[tok(3 * A_WIDTH), tok(2 * A_LORA), tok(A_WIDTH), full(wcomb), full(vecs)],
        out_specs=tok(A_WIDTH), out_shape=jax.ShapeDtypeStruct((bsz, t, A_WIDTH), F32),
        scratch_shapes=scratch,
        compiler_params=_cparams(("parallel", "arbitrary")), name="rwkv7",
    )(rkv, lora, ga, wcomb, vecs)


def _swa_kernel(sink_ref, q_ref, kv_ref, kvp_ref, g_ref, bias_ref, o_ref):
    n = pl.program_id(1)
    q = q_ref[0]
    kv = jnp.concatenate([kvp_ref[0], kv_ref[0]], axis=0)
    kc = kv[:, 0:B_KV_WIDTH]
    vc = kv[:, B_KV_WIDTH:2 * B_KV_WIDTH]
    kr = pltpu.roll(kc, B_HEAD_DIM, axis=1)
    vr = pltpu.roll(vc, B_HEAD_DIM, axis=1)
    h0 = _iota(kc.shape, 1) < B_HEAD_DIM

    def ext(straight, rolled, kvh, e):
        src = straight if kvh == e else rolled
        return jnp.where(h0, src, 0.0) if e == 0 else jnp.where(h0, 0.0, src)

    qi = _iota((BLOCK, 2 * BLOCK), 0)
    kj = _iota((BLOCK, 2 * BLOCK), 1)
    dist = qi + BLOCK - kj
    valid = (dist >= 0) & (dist < WINDOW) & ((n > 0) | (kj >= BLOCK))
    scale = B_HEAD_DIM ** -0.5

    for p in range(B_WIDTH // LANES):
        kvh = (2 * p) // B_GROUP
        qp = q[:, p * LANES:(p + 1) * LANES]
        acc = jnp.zeros((BLOCK, LANES), F32)
        for e in range(2):
            hq = 2 * p + e
            s = _mm_nt(qp, ext(kc, kr, kvh, e)) * scale
            s = jnp.where(valid, s + bias_ref[hq], NEG_INF)
            sink = sink_ref[hq]
            m = jnp.maximum(jnp.max(s, axis=-1, keepdims=True), sink)
            pe = jnp.exp(s - m)
            denom = jnp.sum(pe, axis=-1, keepdims=True) + jnp.exp(sink - m)
            acc = acc + _mm(pe / denom, ext(vc, vr, kvh, e))
        o_ref[0, :, p * LANES:(p + 1) * LANES] = acc * _silu(g_ref[0, :, p * LANES:(p + 1) * LANES])


def _swa(sinks, qb, kvb, gb, bias):
    bsz, t, _ = qb.shape
    nb = t // BLOCK
    return pl.pallas_call(
        _swa_kernel, grid=(bsz, nb),
        in_specs=[pl.BlockSpec(memory_space=pltpu.SMEM),
                  pl.BlockSpec((1, BLOCK, B_WIDTH), lambda b, i: (b, i, 0)),
                  pl.BlockSpec((1, BLOCK, 2 * B_KV_WIDTH), lambda b, i: (b, i, 0)),
                  pl.BlockSpec((1, BLOCK, 2 * B_KV_WIDTH), lambda b, i: (b, jnp.maximum(i - 1, 0), 0)),
                  pl.BlockSpec((1, BLOCK, B_WIDTH), lambda b, i: (b, i, 0)),
                  pl.BlockSpec(bias.shape, lambda b, i: (0, 0, 0))],
        out_specs=pl.BlockSpec((1, BLOCK, B_WIDTH), lambda b, i: (b, i, 0)),
        out_shape=jax.ShapeDtypeStruct((bsz, t, B_WIDTH), F32),
        compiler_params=_cparams(("parallel", "arbitrary")), name="swa",
    )(sinks, qb, kvb, kvb, gb, bias)


def _out_kernel(*refs, n_parts):
    y_refs = refs[:n_parts]
    w_refs = refs[n_parts:2 * n_parts]
    nw_ref, x_ref, o_ref = refs[2 * n_parts:]
    y = None
    for yr, wr in zip(y_refs, w_refs):
        part = jnp.dot(_bf(yr[0]), wr[...], preferred_element_type=F32)
        y = part if y is None else y + part
    yn = y * lax.rsqrt(jnp.mean(y * y, axis=-1, keepdims=True) + NORM_EPS) * nw_ref[...]
    o_ref[0] = x_ref[0] + yn


def _out_proj(ys, ws, nw, x, tm):
    bsz, t, d = x.shape
    full = lambda a: pl.BlockSpec(a.shape, lambda b, i: (0,) * a.ndim)
    tok = lambda n: pl.BlockSpec((1, tm, n), lambda b, i: (b, i, 0))
    return pl.pallas_call(
        functools.partial(_out_kernel, n_parts=len(ys)), grid=(bsz, t // tm),
        in_specs=[tok(y.shape[2]) for y in ys] + [full(w) for w in ws] + [full(nw), tok(d)],
        out_specs=tok(d), out_shape=jax.ShapeDtypeStruct((bsz, t, d), F32),
        compiler_params=_cparams(("parallel", "parallel")), name="out_proj",
    )(*ys, *ws, nw, x)


def _l1_in_kernel(x_ref, nw_ref, w_ref, o_ref):
    x = x_ref[0]
    h = x * lax.rsqrt(jnp.mean(x * x, axis=-1, keepdims=True) + NORM_EPS) * nw_ref[...]
    o_ref[0] = jnp.dot(_bf(h), w_ref[...], preferred_element_type=F32)


def _l1_in(x, nw, w, tm):
    bsz, t, d = x.shape
    n = w.shape[1]
    return pl.pallas_call(
        _l1_in_kernel, grid=(bsz, t // tm),
        in_specs=[pl.BlockSpec((1, tm, d), lambda b, i: (b, i, 0)),
                  pl.BlockSpec(nw.shape, lambda b, i: (0, 0)),
                  pl.BlockSpec(w.shape, lambda b, i: (0, 0))],
        out_specs=pl.BlockSpec((1, tm, n), lambda b, i: (b, i, 0)),
        out_shape=jax.ShapeDtypeStruct((bsz, t, n), F32),
        compiler_params=_cparams(("parallel", "parallel")), name="l1_in",
    )(x, nw, w)


def _hgrn_kernel(q_ref, f_ref, i_ref, g_ref, lb_ref, cw_ref, o_ref, s_ref, qd_sc, ki_sc, dc_sc, *, par):
    ti = pl.program_id(1)

    @pl.when(ti == 0)
    def _():
        s_ref[...] = jnp.zeros_like(s_ref)

    tc = q_ref.shape[1]
    c = CHUNK
    l0 = lb_ref[0:1, :]
    l1 = lb_ref[1:2, :]
    m = jnp.maximum(l0, l1)
    e0 = jnp.exp(l0 - m)
    e1 = jnp.exp(l1 - m)
    s0 = e0 / (e0 + e1)
    s1 = e1 / (e0 + e1)
    lb = (s0 + s1) - s0

    forget = lb + (1.0 - lb) * _sigmoid(f_ref[0])
    gc = _chunk_cumsum(jnp.log(forget), c)
    qd_sc[...] = _silu(q_ref[0]) * jnp.exp(gc)
    ki_sc[...] = (1.0 - forget) * jnp.exp(-gc)
    for j in range(tc // c):
        dc_sc[j * 8:(j + 1) * 8, :] = jnp.broadcast_to(jnp.exp(gc[(j + 1) * c - 1:(j + 1) * c, :]), (8, C_WIDTH))

    causal = _iota((c, c), 1) <= _iota((c, c), 0)
    heads = [slice(h * C_DIM, (h + 1) * C_DIM) for h in range(C_HEADS)]
    chains = [(j, h) for j in range(par) for h in range(C_HEADS)]

    def chunk_body(gi, carry):
        def rows(j):
            return pl.ds(pl.multiple_of((gi * par + j) * c, c), c)

        q_dec = [qd_sc[rows(j), heads[h]] for j, h in chains]
        k_inv = [ki_sc[rows(j), heads[h]] for j, h in chains]
        vv = [i_ref[0, rows(j), heads[h]] for j, h in chains]
        dcs = [dc_sc[pl.ds(pl.multiple_of((gi * par + j) * 8, 8), 1), heads[h]] for j, h in chains]
        sc = [jnp.where(causal, _mm_nt(qd, ki), 0.0) for qd, ki in zip(q_dec, k_inv)]
        upd = [_mm(x.T, ki * dc) for x, ki, dc in zip(vv, k_inv, dcs)]
        intra = [_mm(s, x) for s, x in zip(sc, vv)]
        sts = [s_ref[h] for h in range(C_HEADS)]
        for i, (j, h) in enumerate(chains):
            o_ref[0, rows(j), heads[h]] = intra[i] + _mm_nt(q_dec[i], sts[h])
            sts[h] = sts[h] * dcs[i] + upd[i]
        for h in range(C_HEADS):
            s_ref[h] = sts[h]
        return carry

    lax.fori_loop(0, tc // (c * par), chunk_body, 0)

    cw = cw_ref[...]
    for sl in heads:
        o = o_ref[0, :, sl]
        on = o * lax.rsqrt(jnp.mean(o * o, axis=-1, keepdims=True) + NORM_EPS) * cw
        o_ref[0, :, sl] = on * _silu(g_ref[0, :, sl])


def _hgrn(proj, lbs, cw, tc):
    bsz, t, _ = proj.shape
    par = math.gcd(C_PAR, tc // CHUNK)
    col = lambda j: pl.BlockSpec((1, tc, C_WIDTH), lambda b, i: (b, i, j))
    tile = pltpu.VMEM((tc, C_WIDTH), F32)
    return pl.pallas_call(
        functools.partial(_hgrn_kernel, par=par), grid=(bsz, t // tc),
        in_specs=[col(0), col(1), col(2), col(3),
                  pl.BlockSpec(lbs.shape, lambda b, i: (0, 0)),
                  pl.BlockSpec(cw.shape, lambda b, i: (0, 0))],
        out_specs=pl.BlockSpec((1, tc, C_WIDTH), lambda b, i: (b, i, 0)),
        out_shape=jax.ShapeDtypeStruct((bsz, t, C_WIDTH), F32),
        scratch_shapes=[pltpu.VMEM((C_HEADS, C_DIM, C_DIM), F32), tile, tile,
                        pltpu.VMEM((tc // CHUNK * 8, C_WIDTH), F32)],
        compiler_params=_cparams(("parallel", "arbitrary")), name="hgrn2",
    )(proj, proj, proj, proj, lbs, cw)


def _t5_bucket(dist):
    max_exact = REL_BUCKETS // 2
    d = jnp.maximum(dist, 0)
    large = max_exact + (jnp.log(jnp.maximum(d, 1).astype(F32) / max_exact)
                         / math.log(REL_MAX_DIST / max_exact)
                         * (REL_BUCKETS - max_exact)).astype(jnp.int32)
    large = jnp.minimum(large, REL_BUCKETS - 1)
    return jnp.where(d < max_exact, d, large)


def _tile(t, want):
    return want if t % want == 0 else BLOCK


def kernel(x, rel_bias_table, lower_bounds, l0_pre_norm, l0_post_norm, l0_w_in, l0_w_out, a_mu, a_w0, a_w2, a_a0, a_a2, a_k_k, a_k_a, a_r_k, a_ln_w, a_ln_b, b_sinks, l1_pre_norm, l1_post_norm, l1_w_in, l1_w_out, c_norm_w):
    bsz, t, d = x.shape
    assert d == D_MODEL and t % BLOCK == 0
    row = lambda v: v.reshape(1, -1).astype(F32)

    c_rkv, c_lora, c_ga = 3 * A_WIDTH, 3 * A_WIDTH + 2 * A_LORA, 4 * A_WIDTH + 2 * A_LORA
    c_q, c_kv, c_gb = c_ga + B_WIDTH, c_ga + B_WIDTH + 2 * B_KV_WIDTH, c_ga + 2 * B_WIDTH + 2 * B_KV_WIDTH
    wb = _bf(l0_w_in)
    ws = [wb[:, 0:c_rkv], wb[:, c_rkv:c_lora], wb[:, c_lora:c_ga], wb[:, c_ga:c_q], wb[:, c_q:c_kv], wb[:, c_kv:c_gb]]
    mus = [row(a_mu[0:c_rkv]), row(a_mu[c_rkv:c_lora]), row(a_mu[c_lora:c_ga])]
    zeros = jnp.zeros((A_LORA, A_WIDTH), F32)
    wcomb = jnp.concatenate([jnp.concatenate([a_w2, zeros], axis=1), jnp.concatenate([zeros, a_a2], axis=1)], axis=0)
    vecs = jnp.stack([a_w0, a_a0, a_k_k, a_k_a, a_r_k.reshape(-1), a_ln_w, a_ln_b, jnp.zeros_like(a_w0)]).astype(F32)
    qi = jnp.arange(BLOCK)[:, None]
    kj = jnp.arange(2 * BLOCK)[None, :]
    bias = jnp.transpose(rel_bias_table[_t5_bucket(qi + BLOCK - kj)], (2, 0, 1)).astype(F32)
    w_out0 = _bf(l0_w_out)
    w_in1 = _bf(l1_w_in)
    w_out1 = _bf(l1_w_out)

    tm = _tile(t, 256)
    rkv, lora, ga, qb, kvb, gb = _l0_in(x, row(l0_pre_norm), ws, mus, tm)
    ya = _rwkv(rkv, lora, ga, wcomb, vecs, _tile(t, 512))
    yb = _swa(b_sinks.astype(F32), qb, kvb, gb, bias)
    x1 = _out_proj([ya, yb], [w_out0[0:A_WIDTH], w_out0[A_WIDTH:]], row(l0_post_norm), x, tm)
    proj = _l1_in(x1, row(l1_pre_norm), w_in1, tm)
    o = _hgrn(proj, lower_bounds.astype(F32), row(c_norm_w), _tile(t, 256))
    return _out_proj([o], [w_out1], row(l1_post_norm), x1, tm)
```

```python
import functools
import math

import jax
import jax.numpy as jnp
from jax import lax
from jax.experimental import pallas as pl
from jax.experimental.pallas import tpu as pltpu

D_MODEL = 1024
A_HEAD_DIM = 64
A_WIDTH = 512
A_LORA = 32
A_GN_EPS = 64e-5
B_HEAD_DIM = 64
B_WIDTH = 512
B_HEADS = 8
B_KV_WIDTH = 128
B_GROUP = 4
WINDOW = 128
BLOCK = 128
REL_BUCKETS = 32
REL_MAX_DIST = 128
C_HEADS = 8
C_DIM = 128
C_WIDTH = 1024
CHUNK = 64
NORM_EPS = 1e-6
NEG_INF = -1e30

LANES = 128
MXU_DEPTH = 256
N_PAIRS = A_WIDTH // LANES
A_PAR = 4
C_PAR = 2
VMEM_LIMIT = 48 * 1024 * 1024

F32 = jnp.float32
BF16 = jnp.bfloat16


def _bf(x):
    return x.astype(BF16)


def _mm(a, b):
    return jnp.dot(_bf(a), _bf(b), preferred_element_type=F32)


def _mm_nt(a, b):
    return lax.dot_general(_bf(a), _bf(b), (((1,), (1,)), ((), ())), preferred_element_type=F32)


def _split(x, parts):
    out = []
    rem = x
    for i in range(parts):
        hi = _bf(rem)
        out.append(hi)
        if i + 1 < parts:
            rem = rem - hi.astype(F32)
    return out


def _mm_exact_lhs(a_bf, x, parts):
    n = x.shape[1]
    xs = jnp.concatenate(_split(x, parts), axis=1)
    y = jnp.dot(a_bf, xs, preferred_element_type=F32)
    acc = y[:, 0:n]
    for i in range(1, parts):
        acc = acc + y[:, i * n:(i + 1) * n]
    return acc


def _mm_exact_rhs(x, b_stacked, parts):
    return jnp.dot(jnp.concatenate(_split(x, parts), axis=1), b_stacked, preferred_element_type=F32)


def _mm3(a, b):
    a_hi, a_lo = _split(a, 2)
    b_hi, b_lo = _split(b, 2)
    return jnp.dot(jnp.concatenate([a_hi, a_hi, a_lo], axis=1), jnp.concatenate([b_hi, b_lo, b_hi], axis=0),
                   preferred_element_type=F32)


def _iota(shape, dim):
    return lax.broadcasted_iota(jnp.int32, shape, dim)


def _sigmoid(x):
    return 0.5 * (1.0 + jnp.tanh(0.5 * x))


def _silu(x):
    return x * _sigmoid(x)


def _softplus(z):
    return jnp.maximum(z, 0.0) + jnp.log(1.0 + jnp.exp(-jnp.abs(z)))


def _chunk_cumsum(x, c):
    t = x.shape[0]
    blk = min(t, MXU_DEPTH)
    i = _iota((blk, blk), 0)
    j = _iota((blk, blk), 1)
    cum = jnp.where(((i // c) == (j // c)) & (j <= i), 1.0, 0.0).astype(BF16)
    return jnp.concatenate([_mm_exact_lhs(cum, x[b * blk:(b + 1) * blk], 3) for b in range(t // blk)], axis=0)


def _cparams(sem):
    return pltpu.CompilerParams(dimension_semantics=sem, vmem_limit_bytes=VMEM_LIMIT)


def _l0_in_kernel(x_ref, nw_ref, w_rkv, w_lora, w_ga, w_qb, w_kvb, w_gb, mu_rkv, mu_lora, mu_ga,
                  o_rkv, o_lora, o_ga, o_qb, o_kvb, o_gb, c_rkv, c_lora, c_ga):
    t = pl.program_id(1)

    @pl.when(t == 0)
    def _():
        c_rkv[...] = jnp.zeros_like(c_rkv)
        c_lora[...] = jnp.zeros_like(c_lora)
        c_ga[...] = jnp.zeros_like(c_ga)

    x = x_ref[0]
    h = x * lax.rsqrt(jnp.mean(x * x, axis=-1, keepdims=True) + NORM_EPS) * nw_ref[...]
    hb = _bf(h)
    tm = x.shape[0]

    def shifted(w, mu, carry, out):
        p = jnp.dot(hb, w[...], preferred_element_type=F32)
        prev = pltpu.roll(p, 1, axis=0)
        row = _iota(p.shape, 0)
        prev = jnp.where(row == 0, carry[0:1, :], prev)
        carry[0:1, :] = p[tm - 1:tm, :]
        out[0] = p + mu[...] * (prev - p)

    shifted(w_rkv, mu_rkv, c_rkv, o_rkv)
    shifted(w_lora, mu_lora, c_lora, o_lora)
    shifted(w_ga, mu_ga, c_ga, o_ga)
    o_qb[0] = jnp.dot(hb, w_qb[...], preferred_element_type=F32)
    o_kvb[0] = jnp.dot(hb, w_kvb[...], preferred_element_type=F32)
    o_gb[0] = jnp.dot(hb, w_gb[...], preferred_element_type=F32)


def _l0_in(x, nw, ws, mus, tm):
    bsz, t, d = x.shape
    widths = [w.shape[1] for w in ws]
    full = lambda a: pl.BlockSpec(a.shape, lambda b, i: (0,) * a.ndim)
    in_specs = [pl.BlockSpec((1, tm, d), lambda b, i: (b, i, 0)), full(nw)]
    in_specs += [full(w) for w in ws] + [full(m) for m in mus]
    out_specs = [pl.BlockSpec((1, tm, n), lambda b, i: (b, i, 0)) for n in widths]
    out_shape = [jax.ShapeDtypeStruct((bsz, t, n), F32) for n in widths]
    scratch = [pltpu.VMEM((8, widths[0]), F32), pltpu.VMEM((8, widths[1]), F32), pltpu.VMEM((8, widths[2]), F32)]
    return pl.pallas_call(
        _l0_in_kernel, grid=(bsz, t // tm), in_specs=in_specs, out_specs=out_specs, out_shape=out_shape,
        scratch_shapes=scratch, compiler_params=_cparams(("parallel", "arbitrary")), name="l0_in",
    )(x, nw, *ws, *mus)


def _bd(x):
    h0 = _iota(x.shape, 1) < A_HEAD_DIM
    return jnp.concatenate([jnp.where(h0, x, 0.0), jnp.where(h0, 0.0, x)], axis=0)


def _unit_lower_inverse(ns):
    c = ns[0].shape[0]
    row = _iota(ns[0].shape, 0)
    col = _iota(ns[0].shape, 1) & (A_HEAD_DIM - 1)
    eye = jnp.where(col == row, 1.0, 0.0)
    first = (col == row - 1) & ((row & 1) == 1)
    ts = [eye + jnp.where(first, n, 0.0) for n in ns]
    bs = 2
    while bs < c:
        sel = ((row // (2 * bs)) == (col // (2 * bs))) & (((row // bs) & 1) == 1) & (((col // bs) & 1) == 0)
        xs = [_mm(jnp.where(sel, n, 0.0), _bd(t)) for n, t in zip(ns, ts)]
        ts = [t + _mm(t, _bd(x)) for t, x in zip(ts, xs)]
        bs *= 2
    return ts


def _rwkv_kernel(rkv_ref, lora_ref, ga_ref, wcomb_ref, vec_ref, o_ref,
                 s_ref, ad_sc, rd_sc, bi_sc, ki_sc, dc_sc, q_sc, y_sc, w_sc, bo_sc, ut_sc, vk_sc, *, par):
    ti = pl.program_id(1)

    @pl.when(ti == 0)
    def _():
        s_ref[...] = jnp.zeros_like(s_ref)

    tc = rkv_ref.shape[1]
    c = CHUNK
    n_chunks = tc // c
    w0 = vec_ref[0:1, :]
    a0 = vec_ref[1:2, :]
    k_k = vec_ref[2:3, :]
    k_a = vec_ref[3:4, :]
    r_k = vec_ref[4:5, :]
    ln_w = vec_ref[5:6, :]
    ln_b = vec_ref[6:7, :]

    li = _iota((LANES, LANES), 0) // A_HEAD_DIM
    lj = _iota((LANES, LANES), 1) // A_HEAD_DIM
    bd_mask = li == lj
    head_ones = jnp.where(bd_mask, 1.0, 0.0).astype(BF16)
    head_ones2 = jnp.concatenate([head_ones, head_ones], axis=0)
    pairs = [slice(p * LANES, (p + 1) * LANES) for p in range(N_PAIRS)]

    def headsum(x):
        return jnp.concatenate([_mm_exact_rhs(x[:, sl], head_ones2, 2) for sl in pairs], axis=1)

    r = rkv_ref[0, :, 0:A_WIDTH]
    k = rkv_ref[0, :, A_WIDTH:2 * A_WIDTH]
    v = rkv_ref[0, :, 2 * A_WIDTH:3 * A_WIDTH]
    lo = lora_ref[0]
    z = jnp.where(_iota(lo.shape, 1) < A_LORA, jnp.tanh(lo), lo)
    wa = _mm3(z, wcomb_ref[...])
    logw = -math.exp(-0.5) * _sigmoid(w0 + wa[:, 0:A_WIDTH])
    a = _sigmoid(a0 + wa[:, A_WIDTH:2 * A_WIDTH])
    kk = k * k_k
    kkn = kk * lax.rsqrt(jnp.maximum(headsum(kk * kk), 1e-24))
    k2 = k * (1.0 + (a - 1.0) * k_a)
    g = _chunk_cumsum(logw, c)
    e_neg = jnp.exp(-g)
    ad_sc[...] = -kkn * jnp.exp(g - logw)
    rd_sc[...] = r * jnp.exp(g)
    bi_sc[...] = kkn * a * e_neg
    ki_sc[...] = k2 * e_neg
    for j in range(n_chunks):
        dc_sc[j * 8:(j + 1) * 8, :] = jnp.broadcast_to(jnp.exp(g[(j + 1) * c - 1:(j + 1) * c, :]), (8, A_WIDTH))
    bo_sc[...] = headsum(r * k2 * r_k) * v

    prow = _iota((c, LANES), 0)
    pcol = _iota((c, LANES), 1) & (A_HEAD_DIM - 1)
    strict = pcol < prow
    incl = pcol <= prow
    head0 = _iota((c, LANES), 1) < A_HEAD_DIM
    chains = [(j, p) for j in range(par) for p in range(N_PAIRS)]

    def local_body(gi, carry):
        def rows(j):
            return pl.ds(pl.multiple_of((gi * par + j) * c, c), c)

        a_dec = [ad_sc[rows(j), pairs[p]] for j, p in chains]
        r_dec = [rd_sc[rows(j), pairs[p]] for j, p in chains]
        b_inv = [bi_sc[rows(j), pairs[p]] for j, p in chains]
        k_inv = [ki_sc[rows(j), pairs[p]] for j, p in chains]
        vv = [rkv_ref[0, rows(j), 2 * A_WIDTH + p * LANES:2 * A_WIDTH + (p + 1) * LANES] for j, p in chains]
        k_tail = [ki * dc_sc[pl.ds(pl.multiple_of((gi * par + j) * 8, 8), 1), pairs[p]]
                  for ki, (j, p) in zip(k_inv, chains)]

        gms = [_mm_nt(jnp.concatenate([ad, rd], axis=0),
                      jnp.concatenate([jnp.where(head0, bp, 0.0), jnp.where(head0, 0.0, bp),
                                       jnp.where(head0, kp, 0.0), jnp.where(head0, 0.0, kp)], axis=0))
               for ad, rd, bp, kp in zip(a_dec, r_dec, b_inv, k_inv)]
        n_ab = [jnp.where(strict, gm[0:c, 0:LANES], 0.0) for gm in gms]
        a_ak = [jnp.where(strict, gm[0:c, LANES:2 * LANES], 0.0) for gm in gms]
        a_rb = [jnp.where(incl, gm[c:2 * c, 0:LANES], 0.0) for gm in gms]
        a_rk = [jnp.where(incl, gm[c:2 * c, LANES:2 * LANES], 0.0) for gm in gms]
        tinv = _unit_lower_inverse(n_ab)
        from_v = [_mm(jnp.concatenate([ak, rk], axis=0), _bd(x)) for ak, rk, x in zip(a_ak, a_rk, vv)]
        tw = [_mm(t, jnp.concatenate([_bd(ad), _bd(fv[0:c])], axis=1))
              for t, fv, ad in zip(tinv, from_v, a_dec)]
        ab = [_mm(rb, jnp.concatenate([_bd(x[:, 0:LANES]), _bd(x[:, LANES:2 * LANES])], axis=1))
              for rb, x in zip(a_rb, tw)]
        vk = [_mm(x.T, kt) for x, kt in zip(vv, k_tail)]
        for i, (j, p) in enumerate(chains):
            slot = (gi * par + j) * N_PAIRS + p
            w_sc[rows(j), pairs[p]] = tw[i][:, 0:LANES]
            q_sc[rows(j), pairs[p]] = r_dec[i] + ab[i][:, 0:LANES]
            y_sc[rows(j), pairs[p]] = ab[i][:, LANES:2 * LANES] + from_v[i][c:2 * c]
            ut_sc[slot] = tw[i][:, LANES:2 * LANES].T
            vk_sc[slot] = jnp.where(bd_mask, vk[i], 0.0)
        return carry

    lax.fori_loop(0, n_chunks // par, local_body, 0)

    def state_body(ci, carry):
        r0 = pl.multiple_of(ci * c, c)
        rows = pl.ds(r0, c)
        dc = dc_sc[pl.ds(pl.multiple_of(ci * 8, 8), 1), :]
        sts = [s_ref[p] for p in range(N_PAIRS)]
        uts = [_mm_nt(st, w_sc[rows, sl]) + ut_sc[ci * N_PAIRS + p] for p, (st, sl) in enumerate(zip(sts, pairs))]
        ys = [_mm_nt(q_sc[rows, sl], st) for st, sl in zip(sts, pairs)]
        upd = [_mm(ut, bi_sc[rows, sl] * dc[:, sl]) for ut, sl in zip(uts, pairs)]
        for p, sl in enumerate(pairs):
            s_ref[p] = sts[p] * dc[:, sl] + jnp.where(bd_mask, upd[p], 0.0) + vk_sc[ci * N_PAIRS + p]
            y_sc[rows, sl] = y_sc[rows, sl] + ys[p]
        return carry

    lax.fori_loop(0, n_chunks, state_body, 0)

    y = y_sc[...]
    inv_n = 1.0 / A_HEAD_DIM
    mean = headsum(y) * inv_n
    d = y - mean
    var = headsum(d * d) * inv_n
    yn = d * lax.rsqrt(var + A_GN_EPS) * ln_w + ln_b
    o_ref[0] = ((yn + bo_sc[...]) * _silu(ga_ref[0])).astype(o_ref.dtype)


def _rwkv(rkv, lora, ga, wcomb, vecs, tc):
    bsz, t, _ = rkv.shape
    n_chunks = tc // CHUNK
    par = math.gcd(A_PAR, n_chunks)
    full = lambda a: pl.BlockSpec(a.shape, lambda b, i: (0,) * a.ndim)
    tok = lambda n: pl.BlockSpec((1, tc, n), lambda b, i: (b, i, 0))
    tile = pltpu.VMEM((tc, A_WIDTH), F32)
    scratch = [pltpu.VMEM((N_PAIRS, LANES, LANES), F32),
               tile, tile, tile, tile,
               pltpu.VMEM((n_chunks * 8, A_WIDTH), F32),
               tile, tile, tile, tile,
               pltpu.VMEM((n_chunks * N_PAIRS, LANES, CHUNK), F32),
               pltpu.VMEM((n_chunks * N_PAIRS, LANES, LANES), F32)]
    return pl.pallas_call(
        functools.partial(_rwkv_kernel, par=par), grid=(bsz, t // tc),
        in_specs=[tok(3 * A_WIDTH), tok(2 * A_LORA), tok(A_WIDTH), full(wcomb), full(vecs)],
        out_specs=tok(A_WIDTH), out_shape=jax.ShapeDtypeStruct((bsz, t, A_WIDTH), BF16),
        scratch_shapes=scratch,
        compiler_params=_cparams(("parallel", "arbitrary")), name="rwkv7",
    )(rkv, lora, ga, wcomb, vecs)


def _swa_kernel(sink_ref, q_ref, kv_ref, kvp_ref, g_ref, bias_ref, o_ref):
    q = q_ref[0] * (B_HEAD_DIM ** -0.5)
    kv = jnp.concatenate([kvp_ref[0], kv_ref[0]], axis=0)
    kc = kv[:, 0:B_KV_WIDTH]
    vc = kv[:, B_KV_WIDTH:2 * B_KV_WIDTH]
    kr = pltpu.roll(kc, B_HEAD_DIM, axis=1)
    vr = pltpu.roll(vc, B_HEAD_DIM, axis=1)
    h0 = _iota(kc.shape, 1) < B_HEAD_DIM

    def ext(straight, rolled, kvh, e):
        src = straight if kvh == e else rolled
        return jnp.where(h0, src, 0.0) if e == 0 else jnp.where(h0, 0.0, src)

    heads = [(hq // 2, hq % 2, hq // B_GROUP) for hq in range(B_HEADS)]
    kext = {(kvh, e): _bf(ext(kc, kr, kvh, e)) for _, e, kvh in heads}
    vext = {(kvh, e): _bf(ext(vc, vr, kvh, e)) for _, e, kvh in heads}
    qb = [_bf(q[:, p * LANES:(p + 1) * LANES]) for p in range(B_WIDTH // LANES)]
    s = [lax.dot_general(qb[p], kext[(kvh, e)], (((1,), (1,)), ((), ())), preferred_element_type=F32) + bias_ref[0, hq]
         for hq, (p, e, kvh) in enumerate(heads)]
    m = [jnp.maximum(jnp.max(x, axis=-1, keepdims=True), sink_ref[hq]) for hq, x in enumerate(s)]
    pe = [jnp.exp(x - mx) for x, mx in zip(s, m)]
    inv = [1.0 / (jnp.sum(x, axis=-1, keepdims=True) + jnp.exp(sink_ref[hq] - mx))
           for hq, (x, mx) in enumerate(zip(pe, m))]
    pv = [jnp.dot(_bf(x), vext[(kvh, e)], preferred_element_type=F32) * r
          for x, r, (_, e, kvh) in zip(pe, inv, heads)]
    for p in range(B_WIDTH // LANES):
        sl = slice(p * LANES, (p + 1) * LANES)
        o_ref[0, :, sl] = ((pv[2 * p] + pv[2 * p + 1]) * _silu(g_ref[0, :, sl])).astype(o_ref.dtype)


def _swa(sinks, qb, kvb, gb, bias):
    bsz, t, _ = qb.shape
    nb = t // BLOCK
    return pl.pallas_call(
        _swa_kernel, grid=(bsz, nb),
        in_specs=[pl.BlockSpec(memory_space=pltpu.SMEM),
                  pl.BlockSpec((1, BLOCK, B_WIDTH), lambda b, i: (b, i, 0)),
                  pl.BlockSpec((1, BLOCK, 2 * B_KV_WIDTH), lambda b, i: (b, i, 0)),
                  pl.BlockSpec((1, BLOCK, 2 * B_KV_WIDTH), lambda b, i: (b, jnp.maximum(i - 1, 0), 0)),
                  pl.BlockSpec((1, BLOCK, B_WIDTH), lambda b, i: (b, i, 0)),
                  pl.BlockSpec((1,) + bias.shape[1:], lambda b, i: (jnp.minimum(i, 1), 0, 0, 0))],
        out_specs=pl.BlockSpec((1, BLOCK, B_WIDTH), lambda b, i: (b, i, 0)),
        out_shape=jax.ShapeDtypeStruct((bsz, t, B_WIDTH), BF16),
        compiler_params=_cparams(("parallel", "arbitrary")), name="swa",
    )(sinks, qb, kvb, kvb, gb, bias)


def _out_kernel(*refs, n_parts):
    y_refs = refs[:n_parts]
    w_refs = refs[n_parts:2 * n_parts]
    nw_ref, x_ref, o_ref = refs[2 * n_parts:]
    y = None
    for yr, wr in zip(y_refs, w_refs):
        part = jnp.dot(_bf(yr[0]), wr[...], preferred_element_type=F32)
        y = part if y is None else y + part
    yn = y * lax.rsqrt(jnp.mean(y * y, axis=-1, keepdims=True) + NORM_EPS) * nw_ref[...]
    o_ref[0] = x_ref[0] + yn


def _out_proj(ys, ws, nw, x, tm):
    bsz, t, d = x.shape
    full = lambda a: pl.BlockSpec(a.shape, lambda b, i: (0,) * a.ndim)
    tok = lambda n: pl.BlockSpec((1, tm, n), lambda b, i: (b, i, 0))
    return pl.pallas_call(
        functools.partial(_out_kernel, n_parts=len(ys)), grid=(bsz, t // tm),
        in_specs=[tok(y.shape[2]) for y in ys] + [full(w) for w in ws] + [full(nw), tok(d)],
        out_specs=tok(d), out_shape=jax.ShapeDtypeStruct((bsz, t, d), F32),
        compiler_params=_cparams(("parallel", "parallel")), name="out_proj",
    )(*ys, *ws, nw, x)


def _l1_in_kernel(x_ref, nw_ref, w_ref, lb_ref, qd_ref, ki_ref, v_ref, g_ref, dc_ref):
    x = x_ref[0]
    h = x * lax.rsqrt(jnp.mean(x * x, axis=-1, keepdims=True) + NORM_EPS) * nw_ref[...]
    p = jnp.dot(_bf(h), w_ref[...], preferred_element_type=F32)
    v_ref[0] = p[:, 2 * C_WIDTH:3 * C_WIDTH]
    g_ref[0] = p[:, 3 * C_WIDTH:4 * C_WIDTH]
    l0 = lb_ref[0:1, :]
    l1 = lb_ref[1:2, :]
    m = jnp.maximum(l0, l1)
    e0 = jnp.exp(l0 - m)
    e1 = jnp.exp(l1 - m)
    s0 = e0 / (e0 + e1)
    s1 = e1 / (e0 + e1)
    lb = (s0 + s1) - s0
    forget = lb + (1.0 - lb) * _sigmoid(p[:, C_WIDTH:2 * C_WIDTH])
    gc = _chunk_cumsum(jnp.log(forget), CHUNK)
    qd_ref[0] = _silu(p[:, 0:C_WIDTH]) * jnp.exp(gc)
    ki_ref[0] = (1.0 - forget) * jnp.exp(-gc)
    for j in range(x.shape[0] // CHUNK):
        last = gc[(j + 1) * CHUNK - 1:(j + 1) * CHUNK, :]
        dc_ref[0, j * 8:(j + 1) * 8, :] = jnp.broadcast_to(jnp.exp(last), (8, C_WIDTH))


def _l1_in(x, nw, w, lbs, tm):
    bsz, t, d = x.shape
    tok = pl.BlockSpec((1, tm, C_WIDTH), lambda b, i: (b, i, 0))
    slab = pl.BlockSpec((1, tm // CHUNK * 8, C_WIDTH), lambda b, i: (b, i, 0))
    act = jax.ShapeDtypeStruct((bsz, t, C_WIDTH), F32)
    return pl.pallas_call(
        _l1_in_kernel, grid=(bsz, t // tm),
        in_specs=[pl.BlockSpec((1, tm, d), lambda b, i: (b, i, 0)),
                  pl.BlockSpec(nw.shape, lambda b, i: (0, 0)),
                  pl.BlockSpec(w.shape, lambda b, i: (0, 0)),
                  pl.BlockSpec(lbs.shape, lambda b, i: (0, 0))],
        out_specs=[tok, tok, tok, tok, slab],
        out_shape=[act, act, act, act, jax.ShapeDtypeStruct((bsz, t // CHUNK * 8, C_WIDTH), F32)],
        compiler_params=_cparams(("parallel", "parallel")), name="l1_in",
    )(x, nw, w, lbs)


def _hgrn_kernel(qd_sc, ki_sc, i_ref, g_ref, dc_sc, cw_ref, o_ref, s_ref, o_sc, *, par):
    ti = pl.program_id(1)

    @pl.when(ti == 0)
    def _():
        s_ref[...] = jnp.zeros_like(s_ref)

    tc = i_ref.shape[1]
    c = CHUNK

    causal = _iota((c, c), 1) <= _iota((c, c), 0)
    heads = [slice(h * C_DIM, (h + 1) * C_DIM) for h in range(C_HEADS)]
    chains = [(j, h) for j in range(par) for h in range(C_HEADS)]

    def chunk_body(gi, carry):
        def rows(j):
            return pl.ds(pl.multiple_of((gi * par + j) * c, c), c)

        q_dec = [qd_sc[0, rows(j), heads[h]] for j, h in chains]
        k_inv = [ki_sc[0, rows(j), heads[h]] for j, h in chains]
        vv = [i_ref[0, rows(j), heads[h]] for j, h in chains]
        dcs = [dc_sc[0, pl.ds(pl.multiple_of((gi * par + j) * 8, 8), 1), heads[h]] for j, h in chains]
        sc = [jnp.where(causal, _mm_nt(qd, ki), 0.0) for qd, ki in zip(q_dec, k_inv)]
        upd = [_mm(x.T, ki * dc) for x, ki, dc in zip(vv, k_inv, dcs)]
        intra = [_mm(s, x) for s, x in zip(sc, vv)]
        sts = [s_ref[h] for h in range(C_HEADS)]
        for i, (j, h) in enumerate(chains):
            o_sc[rows(j), heads[h]] = intra[i] + _mm_nt(q_dec[i], sts[h])
            sts[h] = sts[h] * dcs[i] + upd[i]
        for h in range(C_HEADS):
            s_ref[h] = sts[h]
        return carry

    lax.fori_loop(0, tc // (c * par), chunk_body, 0)

    cw = cw_ref[...]
    for sl in heads:
        o = o_sc[:, sl]
        on = o * lax.rsqrt(jnp.mean(o * o, axis=-1, keepdims=True) + NORM_EPS) * cw
        o_ref[0, :, sl] = (on * _silu(g_ref[0, :, sl])).astype(o_ref.dtype)


def _hgrn(qd, ki, vv, gg, dc, cw, tc):
    bsz, t, _ = qd.shape
    par = math.gcd(C_PAR, tc // CHUNK)
    tok = pl.BlockSpec((1, tc, C_WIDTH), lambda b, i: (b, i, 0))
    return pl.pallas_call(
        functools.partial(_hgrn_kernel, par=par), grid=(bsz, t // tc),
        in_specs=[tok, tok, tok, tok,
                  pl.BlockSpec((1, tc // CHUNK * 8, C_WIDTH), lambda b, i: (b, i, 0)),
                  pl.BlockSpec(cw.shape, lambda b, i: (0, 0))],
        out_specs=tok,
        out_shape=jax.ShapeDtypeStruct((bsz, t, C_WIDTH), BF16),
        scratch_shapes=[pltpu.VMEM((C_HEADS, C_DIM, C_DIM), F32), pltpu.VMEM((tc, C_WIDTH), F32)],
        compiler_params=_cparams(("parallel", "arbitrary")), name="hgrn2",
    )(qd, ki, vv, gg, dc, cw)


def _t5_bucket(dist):
    max_exact = REL_BUCKETS // 2
    d = jnp.maximum(dist, 0)
    large = max_exact + (jnp.log(jnp.maximum(d, 1).astype(F32) / max_exact)
                         / math.log(REL_MAX_DIST / max_exact)
                         * (REL_BUCKETS - max_exact)).astype(jnp.int32)
    large = jnp.minimum(large, REL_BUCKETS - 1)
    return jnp.where(d < max_exact, d, large)


def _tile(t, want):
    return want if t % want == 0 else BLOCK


def kernel(x, rel_bias_table, lower_bounds, l0_pre_norm, l0_post_norm, l0_w_in, l0_w_out, a_mu, a_w0, a_w2, a_a0, a_a2, a_k_k, a_k_a, a_r_k, a_ln_w, a_ln_b, b_sinks, l1_pre_norm, l1_post_norm, l1_w_in, l1_w_out, c_norm_w):
    bsz, t, d = x.shape
    assert d == D_MODEL and t % BLOCK == 0
    row = lambda v: v.reshape(1, -1).astype(F32)

    c_rkv, c_lora, c_ga = 3 * A_WIDTH, 3 * A_WIDTH + 2 * A_LORA, 4 * A_WIDTH + 2 * A_LORA
    c_q, c_kv, c_gb = c_ga + B_WIDTH, c_ga + B_WIDTH + 2 * B_KV_WIDTH, c_ga + 2 * B_WIDTH + 2 * B_KV_WIDTH
    wb = _bf(l0_w_in)
    ws = [wb[:, 0:c_rkv], wb[:, c_rkv:c_lora], wb[:, c_lora:c_ga], wb[:, c_ga:c_q], wb[:, c_q:c_kv], wb[:, c_kv:c_gb]]
    mus = [row(a_mu[0:c_rkv]), row(a_mu[c_rkv:c_lora]), row(a_mu[c_lora:c_ga])]
    zeros = jnp.zeros((A_LORA, A_WIDTH), F32)
    wcomb = jnp.concatenate([jnp.concatenate([a_w2, zeros], axis=1), jnp.concatenate([zeros, a_a2], axis=1)], axis=0)
    vecs = jnp.stack([a_w0, a_a0, a_k_k, a_k_a, a_r_k.reshape(-1), a_ln_w, a_ln_b, jnp.zeros_like(a_w0)]).astype(F32)
    qi = jnp.arange(BLOCK)[:, None]
    kj = jnp.arange(2 * BLOCK)[None, :]
    dist = qi + BLOCK - kj
    onehot = (_t5_bucket(dist)[:, :, None] == jnp.arange(REL_BUCKETS)).astype(F32)
    bias = jnp.einsum("qkb,bh->hqk", onehot, rel_bias_table.astype(F32), precision=lax.Precision.HIGHEST)
    in_window = (dist >= 0) & (dist < WINDOW)
    bias = jnp.stack([jnp.where(in_window & (kj >= BLOCK), bias, NEG_INF), jnp.where(in_window, bias, NEG_INF)])
    w_out0 = _bf(l0_w_out)
    w_in1 = _bf(l1_w_in)
    w_out1 = _bf(l1_w_out)

    tm = _tile(t, 256)
    rkv, lora, ga, qb, kvb, gb = _l0_in(x, row(l0_pre_norm), ws, mus, tm)
    ya = _rwkv(rkv, lora, ga, wcomb, vecs, _tile(t, 512))
    yb = _swa(b_sinks.astype(F32), qb, kvb, gb, bias)
    tm_out = _tile(t, 512)
    x1 = _out_proj([ya, yb], [w_out0[0:A_WIDTH], w_out0[A_WIDTH:]], row(l0_post_norm), x, tm_out)
    qd, ki, vv, gg, dc = _l1_in(x1, row(l1_pre_norm), w_in1, lower_bounds.astype(F32), tm)
    o = _hgrn(qd, ki, vv, gg, dc, row(c_norm_w), _tile(t, 256))
    return _out_proj([o], [w_out1], row(l1_post_norm), x1, tm_out)
```

```python
import functools
import math

import jax
import jax.numpy as jnp
from jax import lax
from jax.experimental import pallas as pl
from jax.experimental.pallas import tpu as pltpu

D_MODEL = 1024
A_HEAD_DIM = 64
A_WIDTH = 512
A_LORA = 32
A_GN_EPS = 64e-5
B_HEAD_DIM = 64
B_WIDTH = 512
B_HEADS = 8
B_KV_WIDTH = 128
B_GROUP = 4
WINDOW = 128
BLOCK = 128
REL_BUCKETS = 32
REL_MAX_DIST = 128
C_HEADS = 8
C_DIM = 128
C_WIDTH = 1024
CHUNK = 64
NORM_EPS = 1e-6
NEG_INF = -1e30

LANES = 128
MXU_DEPTH = 256
N_PAIRS = A_WIDTH // LANES
A_PAR = 4
C_PAR = 2
VMEM_LIMIT = 48 * 1024 * 1024

F32 = jnp.float32
BF16 = jnp.bfloat16


def _bf(x):
    return x.astype(BF16)


def _mm(a, b):
    return jnp.dot(_bf(a), _bf(b), preferred_element_type=F32)


def _mm_nt(a, b):
    return lax.dot_general(_bf(a), _bf(b), (((1,), (1,)), ((), ())), preferred_element_type=F32)


def _split(x, parts):
    out = []
    rem = x
    for i in range(parts):
        hi = _bf(rem)
        out.append(hi)
        if i + 1 < parts:
            rem = rem - hi.astype(F32)
    return out


def _mm_exact_lhs(a_bf, x, parts):
    n = x.shape[1]
    xs = jnp.concatenate(_split(x, parts), axis=1)
    y = jnp.dot(a_bf, xs, preferred_element_type=F32)
    acc = y[:, 0:n]
    for i in range(1, parts):
        acc = acc + y[:, i * n:(i + 1) * n]
    return acc


def _mm_exact_rhs(x, b_stacked, parts):
    return jnp.dot(jnp.concatenate(_split(x, parts), axis=1), b_stacked, preferred_element_type=F32)


def _mm3(a, b):
    a_hi, a_lo = _split(a, 2)
    b_hi, b_lo = _split(b, 2)
    return jnp.dot(jnp.concatenate([a_hi, a_hi, a_lo], axis=1), jnp.concatenate([b_hi, b_lo, b_hi], axis=0),
                   preferred_element_type=F32)


def _iota(shape, dim):
    return lax.broadcasted_iota(jnp.int32, shape, dim)


def _sigmoid(x):
    return 0.5 * (1.0 + jnp.tanh(0.5 * x))


def _silu(x):
    return x * _sigmoid(x)


def _softplus(z):
    return jnp.maximum(z, 0.0) + jnp.log(1.0 + jnp.exp(-jnp.abs(z)))


def _chunk_cumsum(x, c):
    t = x.shape[0]
    blk = min(t, MXU_DEPTH)
    i = _iota((blk, blk), 0)
    j = _iota((blk, blk), 1)
    cum = jnp.where(((i // c) == (j // c)) & (j <= i), 1.0, 0.0).astype(BF16)
    return jnp.concatenate([_mm_exact_lhs(cum, x[b * blk:(b + 1) * blk], 3) for b in range(t // blk)], axis=0)


def _cparams(sem):
    return pltpu.CompilerParams(dimension_semantics=sem, vmem_limit_bytes=VMEM_LIMIT)


def _l0_in_kernel(x_ref, nw_ref, w_rkv, w_lora, w_ga, w_qb, w_kvb, w_gb, mu_rkv, mu_lora, mu_ga,
                  o_rkv, o_lora, o_ga, o_qb, o_kvb, o_gb, c_rkv, c_lora, c_ga):
    t = pl.program_id(1)

    @pl.when(t == 0)
    def _():
        c_rkv[...] = jnp.zeros_like(c_rkv)
        c_lora[...] = jnp.zeros_like(c_lora)
        c_ga[...] = jnp.zeros_like(c_ga)

    x = x_ref[0]
    h = x * lax.rsqrt(jnp.mean(x * x, axis=-1, keepdims=True) + NORM_EPS) * nw_ref[...]
    hb = _bf(h)
    tm = x.shape[0]

    def shifted(w, mu, carry, out):
        p = jnp.dot(hb, w[...], preferred_element_type=F32)
        prev = pltpu.roll(p, 1, axis=0)
        row = _iota(p.shape, 0)
        prev = jnp.where(row == 0, carry[0:1, :], prev)
        carry[0:1, :] = p[tm - 1:tm, :]
        out[0] = p + mu[...] * (prev - p)

    shifted(w_rkv, mu_rkv, c_rkv, o_rkv)
    shifted(w_lora, mu_lora, c_lora, o_lora)
    shifted(w_ga, mu_ga, c_ga, o_ga)
    o_qb[0] = jnp.dot(hb, w_qb[...], preferred_element_type=F32)
    o_kvb[0] = jnp.dot(hb, w_kvb[...], preferred_element_type=F32)
    o_gb[0] = jnp.dot(hb, w_gb[...], preferred_element_type=F32)


def _l0_in(x, nw, ws, mus, tm):
    bsz, t, d = x.shape
    widths = [w.shape[1] for w in ws]
    full = lambda a: pl.BlockSpec(a.shape, lambda b, i: (0,) * a.ndim)
    in_specs = [pl.BlockSpec((1, tm, d), lambda b, i: (b, i, 0)), full(nw)]
    in_specs += [full(w) for w in ws] + [full(m) for m in mus]
    out_specs = [pl.BlockSpec((1, tm, n), lambda b, i: (b, i, 0)) for n in widths]
    out_shape = [jax.ShapeDtypeStruct((bsz, t, n), F32) for n in widths]
    scratch = [pltpu.VMEM((8, widths[0]), F32), pltpu.VMEM((8, widths[1]), F32), pltpu.VMEM((8, widths[2]), F32)]
    return pl.pallas_call(
        _l0_in_kernel, grid=(bsz, t // tm), in_specs=in_specs, out_specs=out_specs, out_shape=out_shape,
        scratch_shapes=scratch, compiler_params=_cparams(("parallel", "arbitrary")), name="l0_in",
    )(x, nw, *ws, *mus)


def _bd(x):
    h0 = _iota(x.shape, 1) < A_HEAD_DIM
    return jnp.concatenate([jnp.where(h0, x, 0.0), jnp.where(h0, 0.0, x)], axis=0)


def _unit_lower_inverse(ns):
    c = ns[0].shape[0]
    row = _iota(ns[0].shape, 0)
    col = _iota(ns[0].shape, 1) & (A_HEAD_DIM - 1)
    eye = jnp.where(col == row, 1.0, 0.0)
    first = (col == row - 1) & ((row & 1) == 1)
    ts = [eye + jnp.where(first, n, 0.0) for n in ns]
    bs = 2
    while bs < c:
        sel = ((row // (2 * bs)) == (col // (2 * bs))) & (((row // bs) & 1) == 1) & (((col // bs) & 1) == 0)
        xs = [_mm(jnp.where(sel, n, 0.0), _bd(t)) for n, t in zip(ns, ts)]
        ts = [t + _mm(t, _bd(x)) for t, x in zip(ts, xs)]
        bs *= 2
    return ts


def _rwkv_kernel(rkv_ref, lora_ref, ga_ref, wcomb_ref, vec_ref, o_ref,
                 s_ref, ad_sc, rd_sc, bi_sc, ki_sc, dc_sc, q_sc, y_sc, w_sc, bo_sc, ut_sc, vk_sc, *, par):
    ti = pl.program_id(1)

    @pl.when(ti == 0)
    def _():
        s_ref[...] = jnp.zeros_like(s_ref)

    tc = rkv_ref.shape[1]
    c = CHUNK
    n_chunks = tc // c
    w0 = vec_ref[0:1, :]
    a0 = vec_ref[1:2, :]
    k_k = vec_ref[2:3, :]
    k_a = vec_ref[3:4, :]
    r_k = vec_ref[4:5, :]
    ln_w = vec_ref[5:6, :]
    ln_b = vec_ref[6:7, :]

    li = _iota((LANES, LANES), 0) // A_HEAD_DIM
    lj = _iota((LANES, LANES), 1) // A_HEAD_DIM
    bd_mask = li == lj
    head_ones = jnp.where(bd_mask, 1.0, 0.0).astype(BF16)
    head_ones2 = jnp.concatenate([head_ones, head_ones], axis=0)
    pairs = [slice(p * LANES, (p + 1) * LANES) for p in range(N_PAIRS)]

    def headsum(x):
        return jnp.concatenate([_mm_exact_rhs(x[:, sl], head_ones2, 2) for sl in pairs], axis=1)

    r = rkv_ref[0, :, 0:A_WIDTH]
    k = rkv_ref[0, :, A_WIDTH:2 * A_WIDTH]
    v = rkv_ref[0, :, 2 * A_WIDTH:3 * A_WIDTH]
    lo = lora_ref[0]
    z = jnp.where(_iota(lo.shape, 1) < A_LORA, jnp.tanh(lo), lo)
    wa = _mm3(z, wcomb_ref[...])
    logw = -math.exp(-0.5) * _sigmoid(w0 + wa[:, 0:A_WIDTH])
    a = _sigmoid(a0 + wa[:, A_WIDTH:2 * A_WIDTH])
    kk = k * k_k
    kkn = kk * lax.rsqrt(jnp.maximum(headsum(kk * kk), 1e-24))
    k2 = k * (1.0 + (a - 1.0) * k_a)
    g = _chunk_cumsum(logw, c)
    e_neg = jnp.exp(-g)
    ad_sc[...] = -kkn * jnp.exp(g - logw)
    rd_sc[...] = r * jnp.exp(g)
    bi_sc[...] = kkn * a * e_neg
    ki_sc[...] = k2 * e_neg
    for j in range(n_chunks):
        dc_sc[j * 8:(j + 1) * 8, :] = jnp.broadcast_to(jnp.exp(g[(j + 1) * c - 1:(j + 1) * c, :]), (8, A_WIDTH))
    bo_sc[...] = headsum(r * k2 * r_k) * v

    prow = _iota((c, LANES), 0)
    pcol = _iota((c, LANES), 1) & (A_HEAD_DIM - 1)
    strict = pcol < prow
    incl = pcol <= prow
    head0 = _iota((c, LANES), 1) < A_HEAD_DIM
    chains = [(j, p) for j in range(par) for p in range(N_PAIRS)]

    def local_body(gi, carry):
        def rows(j):
            return pl.ds(pl.multiple_of((gi * par + j) * c, c), c)

        a_dec = [ad_sc[rows(j), pairs[p]] for j, p in chains]
        r_dec = [rd_sc[rows(j), pairs[p]] for j, p in chains]
        b_inv = [bi_sc[rows(j), pairs[p]] for j, p in chains]
        k_inv = [ki_sc[rows(j), pairs[p]] for j, p in chains]
        vv = [rkv_ref[0, rows(j), 2 * A_WIDTH + p * LANES:2 * A_WIDTH + (p + 1) * LANES] for j, p in chains]
        k_tail = [ki * dc_sc[pl.ds(pl.multiple_of((gi * par + j) * 8, 8), 1), pairs[p]]
                  for ki, (j, p) in zip(k_inv, chains)]

        gms = [_mm_nt(jnp.concatenate([ad, rd], axis=0),
                      jnp.concatenate([jnp.where(head0, bp, 0.0), jnp.where(head0, 0.0, bp),
                                       jnp.where(head0, kp, 0.0), jnp.where(head0, 0.0, kp)], axis=0))
               for ad, rd, bp, kp in zip(a_dec, r_dec, b_inv, k_inv)]
        n_ab = [jnp.where(strict, gm[0:c, 0:LANES], 0.0) for gm in gms]
        a_ak = [jnp.where(strict, gm[0:c, LANES:2 * LANES], 0.0) for gm in gms]
        a_rb = [jnp.where(incl, gm[c:2 * c, 0:LANES], 0.0) for gm in gms]
        a_rk = [jnp.where(incl, gm[c:2 * c, LANES:2 * LANES], 0.0) for gm in gms]
        tinv = _unit_lower_inverse(n_ab)
        from_v = [_mm(jnp.concatenate([ak, rk], axis=0), _bd(x)) for ak, rk, x in zip(a_ak, a_rk, vv)]
        tw = [_mm(t, jnp.concatenate([_bd(ad), _bd(fv[0:c])], axis=1))
              for t, fv, ad in zip(tinv, from_v, a_dec)]
        ab = [_mm(rb, jnp.concatenate([_bd(x[:, 0:LANES]), _bd(x[:, LANES:2 * LANES])], axis=1))
              for rb, x in zip(a_rb, tw)]
        vk = [_mm(x.T, kt) for x, kt in zip(vv, k_tail)]
        for i, (j, p) in enumerate(chains):
            slot = (gi * par + j) * N_PAIRS + p
            w_sc[rows(j), pairs[p]] = tw[i][:, 0:LANES]
            q_sc[rows(j), pairs[p]] = r_dec[i] + ab[i][:, 0:LANES]
            y_sc[rows(j), pairs[p]] = ab[i][:, LANES:2 * LANES] + from_v[i][c:2 * c]
            ut_sc[slot] = tw[i][:, LANES:2 * LANES].T
            vk_sc[slot] = jnp.where(bd_mask, vk[i], 0.0)
        return carry

    lax.fori_loop(0, n_chunks // par, local_body, 0)

    def state_body(ci, carry):
        r0 = pl.multiple_of(ci * c, c)
        rows = pl.ds(r0, c)
        dc = dc_sc[pl.ds(pl.multiple_of(ci * 8, 8), 1), :]
        sts = [s_ref[p] for p in range(N_PAIRS)]
        uts = [_mm_nt(st, w_sc[rows, sl]) + ut_sc[ci * N_PAIRS + p] for p, (st, sl) in enumerate(zip(sts, pairs))]
        ys = [_mm_nt(q_sc[rows, sl], st) for st, sl in zip(sts, pairs)]
        upd = [_mm(ut, bi_sc[rows, sl] * dc[:, sl]) for ut, sl in zip(uts, pairs)]
        for p, sl in enumerate(pairs):
            s_ref[p] = sts[p] * dc[:, sl] + jnp.where(bd_mask, upd[p], 0.0) + vk_sc[ci * N_PAIRS + p]
            y_sc[rows, sl] = y_sc[rows, sl] + ys[p]
        return carry

    lax.fori_loop(0, n_chunks, state_body, 0)

    y = y_sc[...]
    inv_n = 1.0 / A_HEAD_DIM
    mean = headsum(y) * inv_n
    d = y - mean
    var = headsum(d * d) * inv_n
    yn = d * lax.rsqrt(var + A_GN_EPS) * ln_w + ln_b
    o_ref[0] = ((yn + bo_sc[...]) * _silu(ga_ref[0])).astype(o_ref.dtype)


def _rwkv(rkv, lora, ga, wcomb, vecs, tc):
    bsz, t, _ = rkv.shape
    n_chunks = tc // CHUNK
    par = math.gcd(A_PAR, n_chunks)
    full = lambda a: pl.BlockSpec(a.shape, lambda b, i: (0,) * a.ndim)
    tok = lambda n: pl.BlockSpec((1, tc, n), lambda b, i: (b, i, 0))
    tile = pltpu.VMEM((tc, A_WIDTH), F32)
    scratch = [pltpu.VMEM((N_PAIRS, LANES, LANES), F32),
               tile, tile, tile, tile,
               pltpu.VMEM((n_chunks * 8, A_WIDTH), F32),
               tile, tile, tile, tile,
               pltpu.VMEM((n_chunks * N_PAIRS, LANES, CHUNK), F32),
               pltpu.VMEM((n_chunks * N_PAIRS, LANES, LANES), F32)]
    return pl.pallas_call(
        functools.partial(_rwkv_kernel, par=par), grid=(bsz, t // tc),
        in_specs=[tok(3 * A_WIDTH), tok(2 * A_LORA), tok(A_WIDTH), full(wcomb), full(vecs)],
        out_specs=tok(A_WIDTH), out_shape=jax.ShapeDtypeStruct((bsz, t, A_WIDTH), BF16),
        scratch_shapes=scratch,
        compiler_params=_cparams(("parallel", "arbitrary")), name="rwkv7",
    )(rkv, lora, ga, wcomb, vecs)


def _swa_kernel(sink_ref, q_ref, kv_ref, kvp_ref, g_ref, bias_ref, o_ref):
    q = q_ref[0] * (B_HEAD_DIM ** -0.5)
    kv = jnp.concatenate([kvp_ref[0], kv_ref[0]], axis=0)
    kc = kv[:, 0:B_KV_WIDTH]
    vc = kv[:, B_KV_WIDTH:2 * B_KV_WIDTH]
    kr = pltpu.roll(kc, B_HEAD_DIM, axis=1)
    vr = pltpu.roll(vc, B_HEAD_DIM, axis=1)
    h0 = _iota(kc.shape, 1) < B_HEAD_DIM

    def ext(straight, rolled, kvh, e):
        src = straight if kvh == e else rolled
        return jnp.where(h0, src, 0.0) if e == 0 else jnp.where(h0, 0.0, src)

    heads = [(hq // 2, hq % 2, hq // B_GROUP) for hq in range(B_HEADS)]
    kext = {(kvh, e): _bf(ext(kc, kr, kvh, e)) for _, e, kvh in heads}
    vext = {(kvh, e): _bf(ext(vc, vr, kvh, e)) for _, e, kvh in heads}
    qb = [_bf(q[:, p * LANES:(p + 1) * LANES]) for p in range(B_WIDTH // LANES)]
    s = [lax.dot_general(qb[p], kext[(kvh, e)], (((1,), (1,)), ((), ())), preferred_element_type=F32) + bias_ref[0, hq]
         for hq, (p, e, kvh) in enumerate(heads)]
    m = [jnp.maximum(jnp.max(x, axis=-1, keepdims=True), sink_ref[hq]) for hq, x in enumerate(s)]
    pe = [jnp.exp(x - mx) for x, mx in zip(s, m)]
    inv = [1.0 / (jnp.sum(x, axis=-1, keepdims=True) + jnp.exp(sink_ref[hq] - mx))
           for hq, (x, mx) in enumerate(zip(pe, m))]
    pv = [jnp.dot(_bf(x), vext[(kvh, e)], preferred_element_type=F32) * r
          for x, r, (_, e, kvh) in zip(pe, inv, heads)]
    for p in range(B_WIDTH // LANES):
        sl = slice(p * LANES, (p + 1) * LANES)
        o_ref[0, :, sl] = ((pv[2 * p] + pv[2 * p + 1]) * _silu(g_ref[0, :, sl])).astype(o_ref.dtype)


def _swa(sinks, qb, kvb, gb, bias):
    bsz, t, _ = qb.shape
    nb = t // BLOCK
    return pl.pallas_call(
        _swa_kernel, grid=(bsz, nb),
        in_specs=[pl.BlockSpec(memory_space=pltpu.SMEM),
                  pl.BlockSpec((1, BLOCK, B_WIDTH), lambda b, i: (b, i, 0)),
                  pl.BlockSpec((1, BLOCK, 2 * B_KV_WIDTH), lambda b, i: (b, i, 0)),
                  pl.BlockSpec((1, BLOCK, 2 * B_KV_WIDTH), lambda b, i: (b, jnp.maximum(i - 1, 0), 0)),
                  pl.BlockSpec((1, BLOCK, B_WIDTH), lambda b, i: (b, i, 0)),
                  pl.BlockSpec((1,) + bias.shape[1:], lambda b, i: (jnp.minimum(i, 1), 0, 0, 0))],
        out_specs=pl.BlockSpec((1, BLOCK, B_WIDTH), lambda b, i: (b, i, 0)),
        out_shape=jax.ShapeDtypeStruct((bsz, t, B_WIDTH), BF16),
        compiler_params=_cparams(("parallel", "arbitrary")), name="swa",
    )(sinks, qb, kvb, kvb, gb, bias)


def _out_kernel(*refs, n_parts):
    y_refs = refs[:n_parts]
    w_refs = refs[n_parts:2 * n_parts]
    nw_ref, x_ref, o_ref = refs[2 * n_parts:]
    y = None
    for yr, wr in zip(y_refs, w_refs):
        part = jnp.dot(_bf(yr[0]), wr[...], preferred_element_type=F32)
        y = part if y is None else y + part
    yn = y * lax.rsqrt(jnp.mean(y * y, axis=-1, keepdims=True) + NORM_EPS) * nw_ref[...]
    o_ref[0] = x_ref[0] + yn


def _out_proj(ys, ws, nw, x, tm):
    bsz, t, d = x.shape
    full = lambda a: pl.BlockSpec(a.shape, lambda b, i: (0,) * a.ndim)
    tok = lambda n: pl.BlockSpec((1, tm, n), lambda b, i: (b, i, 0))
    return pl.pallas_call(
        functools.partial(_out_kernel, n_parts=len(ys)), grid=(bsz, t // tm),
        in_specs=[tok(y.shape[2]) for y in ys] + [full(w) for w in ws] + [full(nw), tok(d)],
        out_specs=tok(d), out_shape=jax.ShapeDtypeStruct((bsz, t, d), F32),
        compiler_params=_cparams(("parallel", "parallel")), name="out_proj",
    )(*ys, *ws, nw, x)


def _l1_in_kernel(x_ref, nw_ref, w_ref, lb_ref, qd_ref, ki_ref, v_ref, g_ref, dc_ref):
    x = x_ref[0]
    h = x * lax.rsqrt(jnp.mean(x * x, axis=-1, keepdims=True) + NORM_EPS) * nw_ref[...]
    p = jnp.dot(_bf(h), w_ref[...], preferred_element_type=F32)
    v_ref[0] = p[:, 2 * C_WIDTH:3 * C_WIDTH]
    g_ref[0] = p[:, 3 * C_WIDTH:4 * C_WIDTH]
    l0 = lb_ref[0:1, :]
    l1 = lb_ref[1:2, :]
    m = jnp.maximum(l0, l1)
    e0 = jnp.exp(l0 - m)
    e1 = jnp.exp(l1 - m)
    s0 = e0 / (e0 + e1)
    s1 = e1 / (e0 + e1)
    lb = (s0 + s1) - s0
    forget = lb + (1.0 - lb) * _sigmoid(p[:, C_WIDTH:2 * C_WIDTH])
    gc = _chunk_cumsum(jnp.log(forget), CHUNK)
    qd_ref[0] = _silu(p[:, 0:C_WIDTH]) * jnp.exp(gc)
    ki_ref[0] = (1.0 - forget) * jnp.exp(-gc)
    for j in range(x.shape[0] // CHUNK):
        last = gc[(j + 1) * CHUNK - 1:(j + 1) * CHUNK, :]
        dc_ref[0, j * 8:(j + 1) * 8, :] = jnp.broadcast_to(jnp.exp(last), (8, C_WIDTH))


def _l1_in(x, nw, w, lbs, tm):
    bsz, t, d = x.shape
    tok = pl.BlockSpec((1, tm, C_WIDTH), lambda b, i: (b, i, 0))
    slab = pl.BlockSpec((1, tm // CHUNK * 8, C_WIDTH), lambda b, i: (b, i, 0))
    act = jax.ShapeDtypeStruct((bsz, t, C_WIDTH), F32)
    return pl.pallas_call(
        _l1_in_kernel, grid=(bsz, t // tm),
        in_specs=[pl.BlockSpec((1, tm, d), lambda b, i: (b, i, 0)),
                  pl.BlockSpec(nw.shape, lambda b, i: (0, 0)),
                  pl.BlockSpec(w.shape, lambda b, i: (0, 0)),
                  pl.BlockSpec(lbs.shape, lambda b, i: (0, 0))],
        out_specs=[tok, tok, tok, tok, slab],
        out_shape=[act, act, act, act, jax.ShapeDtypeStruct((bsz, t // CHUNK * 8, C_WIDTH), F32)],
        compiler_params=_cparams(("parallel", "parallel")), name="l1_in",
    )(x, nw, w, lbs)


def _hgrn_kernel(qd_sc, ki_sc, i_ref, g_ref, dc_sc, cw_ref, o_ref, s_ref, o_sc, *, par):
    ti = pl.program_id(1)

    @pl.when(ti == 0)
    def _():
        s_ref[...] = jnp.zeros_like(s_ref)

    tc = i_ref.shape[1]
    c = CHUNK

    causal = _iota((c, c), 1) <= _iota((c, c), 0)
    heads = [slice(h * C_DIM, (h + 1) * C_DIM) for h in range(C_HEADS)]
    chains = [(j, h) for j in range(par) for h in range(C_HEADS)]

    def chunk_body(gi, carry):
        def rows(j):
            return pl.ds(pl.multiple_of((gi * par + j) * c, c), c)

        q_dec = [qd_sc[0, rows(j), heads[h]] for j, h in chains]
        k_inv = [ki_sc[0, rows(j), heads[h]] for j, h in chains]
        vv = [i_ref[0, rows(j), heads[h]] for j, h in chains]
        dcs = [dc_sc[0, pl.ds(pl.multiple_of((gi * par + j) * 8, 8), 1), heads[h]] for j, h in chains]
        sc = [jnp.where(causal, _mm_nt(qd, ki), 0.0) for qd, ki in zip(q_dec, k_inv)]
        upd = [_mm(x.T, ki * dc) for x, ki, dc in zip(vv, k_inv, dcs)]
        intra = [_mm(s, x) for s, x in zip(sc, vv)]
        sts = [s_ref[h] for h in range(C_HEADS)]
        for i, (j, h) in enumerate(chains):
            o_sc[rows(j), heads[h]] = intra[i] + _mm_nt(q_dec[i], sts[h])
            sts[h] = sts[h] * dcs[i] + upd[i]
        for h in range(C_HEADS):
            s_ref[h] = sts[h]
        return carry

    lax.fori_loop(0, tc // (c * par), chunk_body, 0)

    cw = cw_ref[...]
    for sl in heads:
        o = o_sc[:, sl]
        on = o * lax.rsqrt(jnp.mean(o * o, axis=-1, keepdims=True) + NORM_EPS) * cw
        o_ref[0, :, sl] = (on * _silu(g_ref[0, :, sl])).astype(o_ref.dtype)


def _hgrn(qd, ki, vv, gg, dc, cw, tc):
    bsz, t, _ = qd.shape
    par = math.gcd(C_PAR, tc // CHUNK)
    tok = pl.BlockSpec((1, tc, C_WIDTH), lambda b, i: (b, i, 0))
    return pl.pallas_call(
        functools.partial(_hgrn_kernel, par=par), grid=(bsz, t // tc),
        in_specs=[tok, tok, tok, tok,
                  pl.BlockSpec((1, tc // CHUNK * 8, C_WIDTH), lambda b, i: (b, i, 0)),
                  pl.BlockSpec(cw.shape, lambda b, i: (0, 0))],
        out_specs=tok,
        out_shape=jax.ShapeDtypeStruct((bsz, t, C_WIDTH), BF16),
        scratch_shapes=[pltpu.VMEM((C_HEADS, C_DIM, C_DIM), F32), pltpu.VMEM((tc, C_WIDTH), F32)],
        compiler_params=_cparams(("parallel", "arbitrary")), name="hgrn2",
    )(qd, ki, vv, gg, dc, cw)


def _layer1_kernel(x_ref, nw_ref, w_ref, lb_ref, cw_ref, wo_ref, pnw_ref, o_ref,
                   s_ref, qd_sc, ki_sc, v_sc, g_sc, dc_sc, o_sc, y_sc, *, par):
    ti = pl.program_id(1)

    @pl.when(ti == 0)
    def _():
        s_ref[...] = jnp.zeros_like(s_ref)

    tm = x_ref.shape[1]
    c = CHUNK
    x = x_ref[0]
    h = x * lax.rsqrt(jnp.mean(x * x, axis=-1, keepdims=True) + NORM_EPS) * nw_ref[...]
    p = jnp.dot(_bf(h), w_ref[...], preferred_element_type=F32)
    v_sc[...] = _bf(p[:, 2 * C_WIDTH:3 * C_WIDTH])
    g_sc[...] = p[:, 3 * C_WIDTH:4 * C_WIDTH]
    l0 = lb_ref[0:1, :]
    l1 = lb_ref[1:2, :]
    m = jnp.maximum(l0, l1)
    e0 = jnp.exp(l0 - m)
    e1 = jnp.exp(l1 - m)
    s0 = e0 / (e0 + e1)
    s1 = e1 / (e0 + e1)
    lb = (s0 + s1) - s0
    forget = lb + (1.0 - lb) * _sigmoid(p[:, C_WIDTH:2 * C_WIDTH])
    gc = _chunk_cumsum(jnp.log(forget), c)
    qd_sc[...] = _bf(_silu(p[:, 0:C_WIDTH]) * jnp.exp(gc))
    ki_sc[...] = (1.0 - forget) * jnp.exp(-gc)
    for j in range(tm // c):
        dc_sc[j * 8:(j + 1) * 8, :] = jnp.broadcast_to(jnp.exp(gc[(j + 1) * c - 1:(j + 1) * c, :]), (8, C_WIDTH))

    causal = _iota((c, c), 1) <= _iota((c, c), 0)
    heads = [slice(hd * C_DIM, (hd + 1) * C_DIM) for hd in range(C_HEADS)]
    chains = [(j, hd) for j in range(par) for hd in range(C_HEADS)]

    def chunk_body(gi, carry):
        def rows(j):
            return pl.ds(pl.multiple_of((gi * par + j) * c, c), c)

        q_dec = [qd_sc[rows(j), heads[hd]] for j, hd in chains]
        k_inv = [ki_sc[rows(j), heads[hd]] for j, hd in chains]
        vv = [v_sc[rows(j), heads[hd]] for j, hd in chains]
        dcs = [dc_sc[pl.ds(pl.multiple_of((gi * par + j) * 8, 8), 1), heads[hd]] for j, hd in chains]
        sc = [jnp.where(causal, _mm_nt(qd, ki), 0.0) for qd, ki in zip(q_dec, k_inv)]
        upd = [_mm(vx.astype(F32).T, ki * dc) for vx, ki, dc in zip(vv, k_inv, dcs)]
        intra = [_mm(s, vx) for s, vx in zip(sc, vv)]
        sts = [s_ref[hd] for hd in range(C_HEADS)]
        for i, (j, hd) in enumerate(chains):
            o_sc[rows(j), heads[hd]] = intra[i] + _mm_nt(q_dec[i], sts[hd])
            sts[hd] = sts[hd] * dcs[i] + upd[i]
        for hd in range(C_HEADS):
            s_ref[hd] = sts[hd]
        return carry

    lax.fori_loop(0, tm // (c * par), chunk_body, 0)

    cw = cw_ref[...]
    for sl in heads:
        o = o_sc[:, sl]
        on = o * lax.rsqrt(jnp.mean(o * o, axis=-1, keepdims=True) + NORM_EPS) * cw
        y_sc[:, sl] = _bf(on * _silu(g_sc[:, sl]))
    y = jnp.dot(y_sc[...], wo_ref[...], preferred_element_type=F32)
    yn = y * lax.rsqrt(jnp.mean(y * y, axis=-1, keepdims=True) + NORM_EPS) * pnw_ref[...]
    o_ref[0] = x_ref[0] + yn


def _layer1(x, nw, w, lbs, cw, wo, pnw, tm):
    bsz, t, d = x.shape
    par = math.gcd(C_PAR, tm // CHUNK)
    full = lambda a: pl.BlockSpec(a.shape, lambda b, i: (0,) * a.ndim)
    tok = pl.BlockSpec((1, tm, d), lambda b, i: (b, i, 0))
    act16 = pltpu.VMEM((tm, C_WIDTH), BF16)
    act32 = pltpu.VMEM((tm, C_WIDTH), F32)
    return pl.pallas_call(
        functools.partial(_layer1_kernel, par=par), grid=(bsz, t // tm),
        in_specs=[tok, full(nw), full(w), full(lbs), full(cw), full(wo), full(pnw)],
        out_specs=tok, out_shape=jax.ShapeDtypeStruct((bsz, t, d), F32),
        scratch_shapes=[pltpu.VMEM((C_HEADS, C_DIM, C_DIM), F32),
                        act16, act32, act16, act32,
                        pltpu.VMEM((tm // CHUNK * 8, C_WIDTH), F32),
                        act32, act16],
        compiler_params=_cparams(("parallel", "arbitrary")), name="layer1",
    )(x, nw, w, lbs, cw, wo, pnw)


def _t5_bucket(dist):
    max_exact = REL_BUCKETS // 2
    d = jnp.maximum(dist, 0)
    large = max_exact + (jnp.log(jnp.maximum(d, 1).astype(F32) / max_exact)
                         / math.log(REL_MAX_DIST / max_exact)
                         * (REL_BUCKETS - max_exact)).astype(jnp.int32)
    large = jnp.minimum(large, REL_BUCKETS - 1)
    return jnp.where(d < max_exact, d, large)


def _tile(t, want):
    return want if t % want == 0 else BLOCK


def kernel(x, rel_bias_table, lower_bounds, l0_pre_norm, l0_post_norm, l0_w_in, l0_w_out, a_mu, a_w0, a_w2, a_a0, a_a2, a_k_k, a_k_a, a_r_k, a_ln_w, a_ln_b, b_sinks, l1_pre_norm, l1_post_norm, l1_w_in, l1_w_out, c_norm_w):
    bsz, t, d = x.shape
    assert d == D_MODEL and t % BLOCK == 0
    row = lambda v: v.reshape(1, -1).astype(F32)

    c_rkv, c_lora, c_ga = 3 * A_WIDTH, 3 * A_WIDTH + 2 * A_LORA, 4 * A_WIDTH + 2 * A_LORA
    c_q, c_kv, c_gb = c_ga + B_WIDTH, c_ga + B_WIDTH + 2 * B_KV_WIDTH, c_ga + 2 * B_WIDTH + 2 * B_KV_WIDTH
    wb = _bf(l0_w_in)
    ws = [wb[:, 0:c_rkv], wb[:, c_rkv:c_lora], wb[:, c_lora:c_ga], wb[:, c_ga:c_q], wb[:, c_q:c_kv], wb[:, c_kv:c_gb]]
    mus = [row(a_mu[0:c_rkv]), row(a_mu[c_rkv:c_lora]), row(a_mu[c_lora:c_ga])]
    zeros = jnp.zeros((A_LORA, A_WIDTH), F32)
    wcomb = jnp.concatenate([jnp.concatenate([a_w2, zeros], axis=1), jnp.concatenate([zeros, a_a2], axis=1)], axis=0)
    vecs = jnp.stack([a_w0, a_a0, a_k_k, a_k_a, a_r_k.reshape(-1), a_ln_w, a_ln_b, jnp.zeros_like(a_w0)]).astype(F32)
    qi = jnp.arange(BLOCK)[:, None]
    kj = jnp.arange(2 * BLOCK)[None, :]
    dist = qi + BLOCK - kj
    onehot = (_t5_bucket(dist)[:, :, None] == jnp.arange(REL_BUCKETS)).astype(F32)
    bias = jnp.einsum("qkb,bh->hqk", onehot, rel_bias_table.astype(F32), precision=lax.Precision.HIGHEST)
    in_window = (dist >= 0) & (dist < WINDOW)
    bias = jnp.stack([jnp.where(in_window & (kj >= BLOCK), bias, NEG_INF), jnp.where(in_window, bias, NEG_INF)])
    w_out0 = _bf(l0_w_out)
    w_in1 = _bf(l1_w_in)
    w_out1 = _bf(l1_w_out)

    tm = _tile(t, 256)
    rkv, lora, ga, qb, kvb, gb = _l0_in(x, row(l0_pre_norm), ws, mus, tm)
    ya = _rwkv(rkv, lora, ga, wcomb, vecs, _tile(t, 512))
    yb = _swa(b_sinks.astype(F32), qb, kvb, gb, bias)
    tm_out = _tile(t, 512)
    x1 = _out_proj([ya, yb], [w_out0[0:A_WIDTH], w_out0[A_WIDTH:]], row(l0_post_norm), x, tm_out)
    return _layer1(x1, row(l1_pre_norm), w_in1, lower_bounds.astype(F32), row(c_norm_w), w_out1, row(l1_post_norm), tm)
```

```python
import functools
import math

import jax
import jax.numpy as jnp
from jax import lax
from jax.experimental import pallas as pl
from jax.experimental.pallas import tpu as pltpu

D_MODEL = 1024
A_HEAD_DIM = 64
A_WIDTH = 512
A_LORA = 32
A_GN_EPS = 64e-5
B_HEAD_DIM = 64
B_WIDTH = 512
B_HEADS = 8
B_KV_WIDTH = 128
B_GROUP = 4
WINDOW = 128
BLOCK = 128
REL_BUCKETS = 32
REL_MAX_DIST = 128
C_HEADS = 8
C_DIM = 128
C_WIDTH = 1024
CHUNK = 64
NORM_EPS = 1e-6
NEG_INF = -1e30

LANES = 128
MXU_DEPTH = 256
N_PAIRS = A_WIDTH // LANES
A_PAR = 4
C_PAR = 2
VMEM_LIMIT = 56 * 1024 * 1024

F32 = jnp.float32
BF16 = jnp.bfloat16


def _bf(x):
    return x.astype(BF16)


def _mm(a, b):
    return jnp.dot(_bf(a), _bf(b), preferred_element_type=F32)


def _mm_nt(a, b):
    return lax.dot_general(_bf(a), _bf(b), (((1,), (1,)), ((), ())), preferred_element_type=F32)


def _split(x, parts):
    out = []
    rem = x
    for i in range(parts):
        hi = _bf(rem)
        out.append(hi)
        if i + 1 < parts:
            rem = rem - hi.astype(F32)
    return out


def _mm_exact_lhs(a_bf, x, parts):
    n = x.shape[1]
    xs = jnp.concatenate(_split(x, parts), axis=1)
    y = jnp.dot(a_bf, xs, preferred_element_type=F32)
    acc = y[:, 0:n]
    for i in range(1, parts):
        acc = acc + y[:, i * n:(i + 1) * n]
    return acc


def _mm_exact_rhs(x, b_stacked, parts):
    return jnp.dot(jnp.concatenate(_split(x, parts), axis=1), b_stacked, preferred_element_type=F32)


def _mm3(a, b):
    a_hi, a_lo = _split(a, 2)
    b_hi, b_lo = _split(b, 2)
    return jnp.dot(jnp.concatenate([a_hi, a_hi, a_lo], axis=1), jnp.concatenate([b_hi, b_lo, b_hi], axis=0),
                   preferred_element_type=F32)


def _iota(shape, dim):
    return lax.broadcasted_iota(jnp.int32, shape, dim)


def _sigmoid(x):
    return 0.5 * (1.0 + jnp.tanh(0.5 * x))


def _silu(x):
    return x * _sigmoid(x)


def _chunk_cumsum(x, c):
    t = x.shape[0]
    blk = min(t, MXU_DEPTH)
    i = _iota((blk, blk), 0)
    j = _iota((blk, blk), 1)
    cum = jnp.where(((i // c) == (j // c)) & (j <= i), 1.0, 0.0).astype(BF16)
    return jnp.concatenate([_mm_exact_lhs(cum, x[b * blk:(b + 1) * blk], 3) for b in range(t // blk)], axis=0)


def _cparams(sem):
    return pltpu.CompilerParams(dimension_semantics=sem, vmem_limit_bytes=VMEM_LIMIT)


def _bd(x):
    h0 = _iota(x.shape, 1) < A_HEAD_DIM
    return jnp.concatenate([jnp.where(h0, x, 0.0), jnp.where(h0, 0.0, x)], axis=0)


def _unit_lower_inverse(ns):
    c = ns[0].shape[0]
    row = _iota(ns[0].shape, 0)
    col = _iota(ns[0].shape, 1) & (A_HEAD_DIM - 1)
    eye = jnp.where(col == row, 1.0, 0.0)
    first = (col == row - 1) & ((row & 1) == 1)
    ts = [eye + jnp.where(first, n, 0.0) for n in ns]
    bs = 2
    while bs < c:
        sel = ((row // (2 * bs)) == (col // (2 * bs))) & (((row // bs) & 1) == 1) & (((col // bs) & 1) == 0)
        xs = [_mm(jnp.where(sel, n, 0.0), _bd(t)) for n, t in zip(ns, ts)]
        ts = [t + _mm(t, _bd(x)) for t, x in zip(ts, xs)]
        bs *= 2
    return ts


def _rwkv_tile(r, k, v, lo, wcomb_ref, vec_ref, ga_sc, ya_sc,
               s_ref, v_sc, ad_sc, rd_sc, bi_sc, ki_sc, dc_sc, q_sc, y_sc, w_sc, bo_sc, ut_sc, vk_sc, par):
    tc = r.shape[0]
    c = CHUNK
    n_chunks = tc // c
    w0 = vec_ref[0:1, :]
    a0 = vec_ref[1:2, :]
    k_k = vec_ref[2:3, :]
    k_a = vec_ref[3:4, :]
    r_k = vec_ref[4:5, :]
    ln_w = vec_ref[5:6, :]
    ln_b = vec_ref[6:7, :]

    li = _iota((LANES, LANES), 0) // A_HEAD_DIM
    lj = _iota((LANES, LANES), 1) // A_HEAD_DIM
    bd_mask = li == lj
    head_ones = jnp.where(bd_mask, 1.0, 0.0).astype(BF16)
    head_ones2 = jnp.concatenate([head_ones, head_ones], axis=0)
    pairs = [slice(p * LANES, (p + 1) * LANES) for p in range(N_PAIRS)]

    def headsum(x):
        return jnp.concatenate([_mm_exact_rhs(x[:, sl], head_ones2, 2) for sl in pairs], axis=1)

    z = jnp.where(_iota(lo.shape, 1) < A_LORA, jnp.tanh(lo), lo)
    wa = _mm3(z, wcomb_ref[...])
    logw = -math.exp(-0.5) * _sigmoid(w0 + wa[:, 0:A_WIDTH])
    a = _sigmoid(a0 + wa[:, A_WIDTH:2 * A_WIDTH])
    kk = k * k_k
    kkn = kk * lax.rsqrt(jnp.maximum(headsum(kk * kk), 1e-24))
    k2 = k * (1.0 + (a - 1.0) * k_a)
    g = _chunk_cumsum(logw, c)
    e_neg = jnp.exp(-g)
    v_sc[...] = v
    ad_sc[...] = -kkn * jnp.exp(g - logw)
    rd_sc[...] = r * jnp.exp(g)
    bi_sc[...] = kkn * a * e_neg
    ki_sc[...] = k2 * e_neg
    for j in range(n_chunks):
        dc_sc[j * 8:(j + 1) * 8, :] = jnp.broadcast_to(jnp.exp(g[(j + 1) * c - 1:(j + 1) * c, :]), (8, A_WIDTH))
    bo_sc[...] = headsum(r * k2 * r_k) * v

    prow = _iota((c, LANES), 0)
    pcol = _iota((c, LANES), 1) & (A_HEAD_DIM - 1)
    strict = pcol < prow
    incl = pcol <= prow
    head0 = _iota((c, LANES), 1) < A_HEAD_DIM
    chains = [(j, p) for j in range(par) for p in range(N_PAIRS)]

    def local_body(gi, carry):
        def rows(j):
            return pl.ds(pl.multiple_of((gi * par + j) * c, c), c)

        a_dec = [ad_sc[rows(j), pairs[p]] for j, p in chains]
        r_dec = [rd_sc[rows(j), pairs[p]] for j, p in chains]
        b_inv = [bi_sc[rows(j), pairs[p]] for j, p in chains]
        k_inv = [ki_sc[rows(j), pairs[p]] for j, p in chains]
        vv = [v_sc[rows(j), pairs[p]] for j, p in chains]
        k_tail = [ki * dc_sc[pl.ds(pl.multiple_of((gi * par + j) * 8, 8), 1), pairs[p]]
                  for ki, (j, p) in zip(k_inv, chains)]

        gms = [_mm_nt(jnp.concatenate([ad, rd], axis=0),
                      jnp.concatenate([jnp.where(head0, bp, 0.0), jnp.where(head0, 0.0, bp),
                                       jnp.where(head0, kp, 0.0), jnp.where(head0, 0.0, kp)], axis=0))
               for ad, rd, bp, kp in zip(a_dec, r_dec, b_inv, k_inv)]
        n_ab = [jnp.where(strict, gm[0:c, 0:LANES], 0.0) for gm in gms]
        a_ak = [jnp.where(strict, gm[0:c, LANES:2 * LANES], 0.0) for gm in gms]
        a_rb = [jnp.where(incl, gm[c:2 * c, 0:LANES], 0.0) for gm in gms]
        a_rk = [jnp.where(incl, gm[c:2 * c, LANES:2 * LANES], 0.0) for gm in gms]
        tinv = _unit_lower_inverse(n_ab)
        from_v = [_mm(jnp.concatenate([ak, rk], axis=0), _bd(x)) for ak, rk, x in zip(a_ak, a_rk, vv)]
        tw = [_mm(t, jnp.concatenate([_bd(ad), _bd(fv[0:c])], axis=1))
              for t, fv, ad in zip(tinv, from_v, a_dec)]
        ab = [_mm(rb, jnp.concatenate([_bd(x[:, 0:LANES]), _bd(x[:, LANES:2 * LANES])], axis=1))
              for rb, x in zip(a_rb, tw)]
        vk = [_mm(x.T, kt) for x, kt in zip(vv, k_tail)]
        for i, (j, p) in enumerate(chains):
            slot = (gi * par + j) * N_PAIRS + p
            w_sc[rows(j), pairs[p]] = tw[i][:, 0:LANES]
            q_sc[rows(j), pairs[p]] = r_dec[i] + ab[i][:, 0:LANES]
            y_sc[rows(j), pairs[p]] = ab[i][:, LANES:2 * LANES] + from_v[i][c:2 * c]
            ut_sc[slot] = tw[i][:, LANES:2 * LANES].T
            vk_sc[slot] = jnp.where(bd_mask, vk[i], 0.0)
        return carry

    lax.fori_loop(0, n_chunks // par, local_body, 0)

    def state_body(ci, carry):
        r0 = pl.multiple_of(ci * c, c)
        rows = pl.ds(r0, c)
        dc = dc_sc[pl.ds(pl.multiple_of(ci * 8, 8), 1), :]
        sts = [s_ref[p] for p in range(N_PAIRS)]
        uts = [_mm_nt(st, w_sc[rows, sl]) + ut_sc[ci * N_PAIRS + p] for p, (st, sl) in enumerate(zip(sts, pairs))]
        ys = [_mm_nt(q_sc[rows, sl], st) for st, sl in zip(sts, pairs)]
        upd = [_mm(ut, bi_sc[rows, sl] * dc[:, sl]) for ut, sl in zip(uts, pairs)]
        for p, sl in enumerate(pairs):
            s_ref[p] = sts[p] * dc[:, sl] + jnp.where(bd_mask, upd[p], 0.0) + vk_sc[ci * N_PAIRS + p]
            y_sc[rows, sl] = y_sc[rows, sl] + ys[p]
        return carry

    lax.fori_loop(0, n_chunks, state_body, 0)

    y = y_sc[...]
    inv_n = 1.0 / A_HEAD_DIM
    mean = headsum(y) * inv_n
    d = y - mean
    var = headsum(d * d) * inv_n
    yn = d * lax.rsqrt(var + A_GN_EPS) * ln_w + ln_b
    ya_sc[...] = _bf((yn + bo_sc[...]) * _silu(ga_sc[...]))


def _swa_tile(first_tile, sink_ref, bias_ref, qb_sc, kv_sc, gb_sc, yb_sc):
    n_blocks = qb_sc.shape[0] // BLOCK
    h0 = _iota((2 * BLOCK, B_KV_WIDTH), 1) < B_HEAD_DIM
    heads = [(hq // 2, hq % 2, hq // B_GROUP) for hq in range(B_HEADS)]

    def ext(straight, rolled, kvh, e):
        src = straight if kvh == e else rolled
        return _bf(jnp.where(h0, src, 0.0) if e == 0 else jnp.where(h0, 0.0, src))

    def block_body(j, carry):
        r0 = pl.multiple_of(j * BLOCK, BLOCK)
        rows = pl.ds(r0, BLOCK)
        variant = jnp.where(first_tile & (j == 0), 0, 1)
        kv = kv_sc[pl.ds(r0, 2 * BLOCK), :]
        kc = kv[:, 0:B_KV_WIDTH]
        vc = kv[:, B_KV_WIDTH:2 * B_KV_WIDTH]
        kr = pltpu.roll(kc, B_HEAD_DIM, axis=1)
        vr = pltpu.roll(vc, B_HEAD_DIM, axis=1)
        kext = {(kvh, e): ext(kc, kr, kvh, e) for _, e, kvh in heads}
        vext = {(kvh, e): ext(vc, vr, kvh, e) for _, e, kvh in heads}
        q = [qb_sc[rows, p * LANES:(p + 1) * LANES] for p in range(B_WIDTH // LANES)]
        s = [lax.dot_general(q[p], kext[(kvh, e)], (((1,), (1,)), ((), ())), preferred_element_type=F32)
             + bias_ref[variant, hq] for hq, (p, e, kvh) in enumerate(heads)]
        m = [jnp.maximum(jnp.max(x, axis=-1, keepdims=True), sink_ref[hq]) for hq, x in enumerate(s)]
        pe = [jnp.exp(x - mx) for x, mx in zip(s, m)]
        inv = [1.0 / (jnp.sum(x, axis=-1, keepdims=True) + jnp.exp(sink_ref[hq] - mx))
               for hq, (x, mx) in enumerate(zip(pe, m))]
        pv = [jnp.dot(_bf(x), vext[(kvh, e)], preferred_element_type=F32) * rcp
              for x, rcp, (_, e, kvh) in zip(pe, inv, heads)]
        for p in range(B_WIDTH // LANES):
            sl = slice(p * LANES, (p + 1) * LANES)
            yb_sc[rows, sl] = _bf((pv[2 * p] + pv[2 * p + 1]) * _silu(gb_sc[rows, sl]))
        return carry

    lax.fori_loop(0, n_blocks, block_body, 0)


def _layer0_kernel(sink_ref, x_ref, nw_ref, w_rkv, w_lora, w_ga, w_qb, w_kvb, w_gb, mu_rkv, mu_lora, mu_ga,
                   wcomb_ref, vec_ref, bias_ref, wo_a, wo_b, pnw_ref, o_ref,
                   c_rkv, c_lora, c_ga, ga_sc, qb_sc, kv_sc, gb_sc, ya_sc, yb_sc,
                   s_ref, v_sc, ad_sc, rd_sc, bi_sc, ki_sc, dc_sc, q_sc, y_sc, w_sc, bo_sc, ut_sc, vk_sc, *, par):
    ti = pl.program_id(1)
    tc = x_ref.shape[1]

    @pl.when(ti == 0)
    def _():
        c_rkv[...] = jnp.zeros_like(c_rkv)
        c_lora[...] = jnp.zeros_like(c_lora)
        c_ga[...] = jnp.zeros_like(c_ga)
        s_ref[...] = jnp.zeros_like(s_ref)
        kv_sc[0:BLOCK, :] = jnp.zeros((BLOCK, 2 * B_KV_WIDTH), F32)

    x = x_ref[0]
    h = x * lax.rsqrt(jnp.mean(x * x, axis=-1, keepdims=True) + NORM_EPS) * nw_ref[...]
    hb = _bf(h)

    def shifted(w, mu, carry):
        p = jnp.dot(hb, w[...], preferred_element_type=F32)
        prev = pltpu.roll(p, 1, axis=0)
        prev = jnp.where(_iota(p.shape, 0) == 0, carry[0:1, :], prev)
        carry[0:1, :] = p[tc - 1:tc, :]
        return p + mu[...] * (prev - p)

    rkv = shifted(w_rkv, mu_rkv, c_rkv)
    lo = shifted(w_lora, mu_lora, c_lora)
    ga_sc[...] = shifted(w_ga, mu_ga, c_ga)
    qb_sc[...] = _bf(jnp.dot(hb, w_qb[...], preferred_element_type=F32) * (B_HEAD_DIM ** -0.5))
    kv_sc[BLOCK:BLOCK + tc, :] = jnp.dot(hb, w_kvb[...], preferred_element_type=F32)
    gb_sc[...] = jnp.dot(hb, w_gb[...], preferred_element_type=F32)

    _rwkv_tile(rkv[:, 0:A_WIDTH], rkv[:, A_WIDTH:2 * A_WIDTH], rkv[:, 2 * A_WIDTH:3 * A_WIDTH], lo,
               wcomb_ref, vec_ref, ga_sc, ya_sc,
               s_ref, v_sc, ad_sc, rd_sc, bi_sc, ki_sc, dc_sc, q_sc, y_sc, w_sc, bo_sc, ut_sc, vk_sc, par)
    _swa_tile(ti == 0, sink_ref, bias_ref, qb_sc, kv_sc, gb_sc, yb_sc)
    kv_sc[0:BLOCK, :] = kv_sc[tc:tc + BLOCK, :]

    y = (jnp.dot(ya_sc[...], wo_a[...], preferred_element_type=F32)
         + jnp.dot(yb_sc[...], wo_b[...], preferred_element_type=F32))
    yn = y * lax.rsqrt(jnp.mean(y * y, axis=-1, keepdims=True) + NORM_EPS) * pnw_ref[...]
    o_ref[0] = x_ref[0] + yn


def _layer0(sinks, x, nw, ws, mus, wcomb, vecs, bias, wo_a, wo_b, pnw, tc):
    bsz, t, d = x.shape
    n_chunks = tc // CHUNK
    par = math.gcd(A_PAR, n_chunks)
    full = lambda a: pl.BlockSpec(a.shape, lambda b, i: (0,) * a.ndim)
    tok = pl.BlockSpec((1, tc, d), lambda b, i: (b, i, 0))
    widths = [w.shape[1] for w in ws]
    tile = pltpu.VMEM((tc, A_WIDTH), F32)
    tile16 = pltpu.VMEM((tc, A_WIDTH), BF16)
    scratch = [pltpu.VMEM((8, widths[0]), F32), pltpu.VMEM((8, widths[1]), F32), pltpu.VMEM((8, widths[2]), F32),
               tile,
               tile16,
               pltpu.VMEM((tc + BLOCK, 2 * B_KV_WIDTH), F32),
               tile,
               tile16, tile16,
               pltpu.VMEM((N_PAIRS, LANES, LANES), F32),
               tile, tile, tile, tile, tile,
               pltpu.VMEM((n_chunks * 8, A_WIDTH), F32),
               tile, tile, tile, tile,
               pltpu.VMEM((n_chunks * N_PAIRS, LANES, CHUNK), F32),
               pltpu.VMEM((n_chunks * N_PAIRS, LANES, LANES), F32)]
    return pl.pallas_call(
        functools.partial(_layer0_kernel, par=par), grid=(bsz, t // tc),
        in_specs=[pl.BlockSpec(memory_space=pltpu.SMEM), tok, full(nw)] + [full(w) for w in ws]
                 + [full(m) for m in mus] + [full(wcomb), full(vecs), full(bias), full(wo_a), full(wo_b), full(pnw)],
        out_specs=tok, out_shape=jax.ShapeDtypeStruct((bsz, t, d), F32),
        scratch_shapes=scratch,
        compiler_params=_cparams(("parallel", "arbitrary")), name="layer0",
    )(sinks, x, nw, *ws, *mus, wcomb, vecs, bias, wo_a, wo_b, pnw)


def _layer1_kernel(x_ref, nw_ref, w_ref, lb_ref, cw_ref, wo_ref, pnw_ref, o_ref,
                   s_ref, qd_sc, ki_sc, v_sc, g_sc, dc_sc, o_sc, y_sc, *, par):
    ti = pl.program_id(1)

    @pl.when(ti == 0)
    def _():
        s_ref[...] = jnp.zeros_like(s_ref)

    tm = x_ref.shape[1]
    c = CHUNK
    x = x_ref[0]
    h = x * lax.rsqrt(jnp.mean(x * x, axis=-1, keepdims=True) + NORM_EPS) * nw_ref[...]
    p = jnp.dot(_bf(h), w_ref[...], preferred_element_type=F32)
    v_sc[...] = _bf(p[:, 2 * C_WIDTH:3 * C_WIDTH])
    g_sc[...] = p[:, 3 * C_WIDTH:4 * C_WIDTH]
    l0 = lb_ref[0:1, :]
    l1 = lb_ref[1:2, :]
    m = jnp.maximum(l0, l1)
    e0 = jnp.exp(l0 - m)
    e1 = jnp.exp(l1 - m)
    s0 = e0 / (e0 + e1)
    s1 = e1 / (e0 + e1)
    lb = (s0 + s1) - s0
    forget = lb + (1.0 - lb) * _sigmoid(p[:, C_WIDTH:2 * C_WIDTH])
    gc = _chunk_cumsum(jnp.log(forget), c)
    qd_sc[...] = _bf(_silu(p[:, 0:C_WIDTH]) * jnp.exp(gc))
    ki_sc[...] = (1.0 - forget) * jnp.exp(-gc)
    for j in range(tm // c):
        dc_sc[j * 8:(j + 1) * 8, :] = jnp.broadcast_to(jnp.exp(gc[(j + 1) * c - 1:(j + 1) * c, :]), (8, C_WIDTH))

    causal = _iota((c, c), 1) <= _iota((c, c), 0)
    heads = [slice(hd * C_DIM, (hd + 1) * C_DIM) for hd in range(C_HEADS)]
    chains = [(j, hd) for j in range(par) for hd in range(C_HEADS)]

    def chunk_body(gi, carry):
        def rows(j):
            return pl.ds(pl.multiple_of((gi * par + j) * c, c), c)

        q_dec = [qd_sc[rows(j), heads[hd]] for j, hd in chains]
        k_inv = [ki_sc[rows(j), heads[hd]] for j, hd in chains]
        vv = [v_sc[rows(j), heads[hd]] for j, hd in chains]
        dcs = [dc_sc[pl.ds(pl.multiple_of((gi * par + j) * 8, 8), 1), heads[hd]] for j, hd in chains]
        sc = [jnp.where(causal, _mm_nt(qd, ki), 0.0) for qd, ki in zip(q_dec, k_inv)]
        upd = [_mm(vx.astype(F32).T, ki * dc) for vx, ki, dc in zip(vv, k_inv, dcs)]
        intra = [_mm(s, vx) for s, vx in zip(sc, vv)]
        sts = [s_ref[hd] for hd in range(C_HEADS)]
        for i, (j, hd) in enumerate(chains):
            o_sc[rows(j), heads[hd]] = intra[i] + _mm_nt(q_dec[i], sts[hd])
            sts[hd] = sts[hd] * dcs[i] + upd[i]
        for hd in range(C_HEADS):
            s_ref[hd] = sts[hd]
        return carry

    lax.fori_loop(0, tm // (c * par), chunk_body, 0)

    cw = cw_ref[...]
    for sl in heads:
        o = o_sc[:, sl]
        on = o * lax.rsqrt(jnp.mean(o * o, axis=-1, keepdims=True) + NORM_EPS) * cw
        y_sc[:, sl] = _bf(on * _silu(g_sc[:, sl]))
    y = jnp.dot(y_sc[...], wo_ref[...], preferred_element_type=F32)
    yn = y * lax.rsqrt(jnp.mean(y * y, axis=-1, keepdims=True) + NORM_EPS) * pnw_ref[...]
    o_ref[0] = x_ref[0] + yn


def _layer1(x, nw, w, lbs, cw, wo, pnw, tm):
    bsz, t, d = x.shape
    par = math.gcd(C_PAR, tm // CHUNK)
    full = lambda a: pl.BlockSpec(a.shape, lambda b, i: (0,) * a.ndim)
    tok = pl.BlockSpec((1, tm, d), lambda b, i: (b, i, 0))
    act16 = pltpu.VMEM((tm, C_WIDTH), BF16)
    act32 = pltpu.VMEM((tm, C_WIDTH), F32)
    return pl.pallas_call(
        functools.partial(_layer1_kernel, par=par), grid=(bsz, t // tm),
        in_specs=[tok, full(nw), full(w), full(lbs), full(cw), full(wo), full(pnw)],
        out_specs=tok, out_shape=jax.ShapeDtypeStruct((bsz, t, d), F32),
        scratch_shapes=[pltpu.VMEM((C_HEADS, C_DIM, C_DIM), F32),
                        act16, act32, act16, act32,
                        pltpu.VMEM((tm // CHUNK * 8, C_WIDTH), F32),
                        act32, act16],
        compiler_params=_cparams(("parallel", "arbitrary")), name="layer1",
    )(x, nw, w, lbs, cw, wo, pnw)


def _t5_bucket(dist):
    max_exact = REL_BUCKETS // 2
    d = jnp.maximum(dist, 0)
    large = max_exact + (jnp.log(jnp.maximum(d, 1).astype(F32) / max_exact)
                         / math.log(REL_MAX_DIST / max_exact)
                         * (REL_BUCKETS - max_exact)).astype(jnp.int32)
    large = jnp.minimum(large, REL_BUCKETS - 1)
    return jnp.where(d < max_exact, d, large)


def _tile(t, want):
    return want if t % want == 0 else BLOCK


def kernel(x, rel_bias_table, lower_bounds, l0_pre_norm, l0_post_norm, l0_w_in, l0_w_out, a_mu, a_w0, a_w2, a_a0, a_a2, a_k_k, a_k_a, a_r_k, a_ln_w, a_ln_b, b_sinks, l1_pre_norm, l1_post_norm, l1_w_in, l1_w_out, c_norm_w):
    bsz, t, d = x.shape
    assert d == D_MODEL and t % BLOCK == 0
    row = lambda v: v.reshape(1, -1).astype(F32)

    c_rkv, c_lora, c_ga = 3 * A_WIDTH, 3 * A_WIDTH + 2 * A_LORA, 4 * A_WIDTH + 2 * A_LORA
    c_q, c_kv, c_gb = c_ga + B_WIDTH, c_ga + B_WIDTH + 2 * B_KV_WIDTH, c_ga + 2 * B_WIDTH + 2 * B_KV_WIDTH
    wb = _bf(l0_w_in)
    ws = [wb[:, 0:c_rkv], wb[:, c_rkv:c_lora], wb[:, c_lora:c_ga], wb[:, c_ga:c_q], wb[:, c_q:c_kv], wb[:, c_kv:c_gb]]
    mus = [row(a_mu[0:c_rkv]), row(a_mu[c_rkv:c_lora]), row(a_mu[c_lora:c_ga])]
    zeros = jnp.zeros((A_LORA, A_WIDTH), F32)
    wcomb = jnp.concatenate([jnp.concatenate([a_w2, zeros], axis=1), jnp.concatenate([zeros, a_a2], axis=1)], axis=0)
    vecs = jnp.stack([a_w0, a_a0, a_k_k, a_k_a, a_r_k.reshape(-1), a_ln_w, a_ln_b, jnp.zeros_like(a_w0)]).astype(F32)
    qi = jnp.arange(BLOCK)[:, None]
    kj = jnp.arange(2 * BLOCK)[None, :]
    dist = qi + BLOCK - kj
    onehot = (_t5_bucket(dist)[:, :, None] == jnp.arange(REL_BUCKETS)).astype(F32)
    bias = jnp.einsum("qkb,bh->hqk", onehot, rel_bias_table.astype(F32), precision=lax.Precision.HIGHEST)
    in_window = (dist >= 0) & (dist < WINDOW)
    bias = jnp.stack([jnp.where(in_window & (kj >= BLOCK), bias, NEG_INF), jnp.where(in_window, bias, NEG_INF)])
    w_out0 = _bf(l0_w_out)
    w_in1 = _bf(l1_w_in)
    w_out1 = _bf(l1_w_out)

    x1 = _layer0(b_sinks.astype(F32), x, row(l0_pre_norm), ws, mus, wcomb, vecs, bias,
                 w_out0[0:A_WIDTH], w_out0[A_WIDTH:], row(l0_post_norm), _tile(t, 512))
    return _layer1(x1, row(l1_pre_norm), w_in1, lower_bounds.astype(F32), row(c_norm_w), w_out1, row(l1_post_norm),
                   _tile(t, 256))
```

```python
import functools
import math

import jax
import jax.numpy as jnp
from jax import lax
from jax.experimental import pallas as pl
from jax.experimental.pallas import tpu as pltpu

D_MODEL = 1024
A_HEAD_DIM = 64
A_WIDTH = 512
A_LORA = 32
A_GN_EPS = 64e-5
B_HEAD_DIM = 64
B_WIDTH = 512
B_HEADS = 8
B_KV_WIDTH = 128
B_GROUP = 4
WINDOW = 128
BLOCK = 128
REL_BUCKETS = 32
REL_MAX_DIST = 128
C_HEADS = 8
C_DIM = 128
C_WIDTH = 1024
CHUNK = 64
NORM_EPS = 1e-6
NEG_INF = -1e30

LANES = 128
MXU_DEPTH = 256
N_PAIRS = A_WIDTH // LANES
A_PAR = 4
C_PAR = 2
VMEM_LIMIT = 56 * 1024 * 1024

F32 = jnp.float32
BF16 = jnp.bfloat16


def _bf(x):
    return x.astype(BF16)


def _mm(a, b):
    return jnp.dot(_bf(a), _bf(b), preferred_element_type=F32)


def _mm_nt(a, b):
    return lax.dot_general(_bf(a), _bf(b), (((1,), (1,)), ((), ())), preferred_element_type=F32)


def _split(x, parts):
    out = []
    rem = x
    for i in range(parts):
        hi = _bf(rem)
        out.append(hi)
        if i + 1 < parts:
            rem = rem - hi.astype(F32)
    return out


def _mm_exact_lhs(a_bf, x, parts):
    n = x.shape[1]
    xs = jnp.concatenate(_split(x, parts), axis=1)
    y = jnp.dot(a_bf, xs, preferred_element_type=F32)
    acc = y[:, 0:n]
    for i in range(1, parts):
        acc = acc + y[:, i * n:(i + 1) * n]
    return acc


def _mm_exact_rhs(x, b_stacked, parts):
    return jnp.dot(jnp.concatenate(_split(x, parts), axis=1), b_stacked, preferred_element_type=F32)


def _mm3(a, b):
    a_hi, a_lo = _split(a, 2)
    b_hi, b_lo = _split(b, 2)
    return jnp.dot(jnp.concatenate([a_hi, a_hi, a_lo], axis=1), jnp.concatenate([b_hi, b_lo, b_hi], axis=0),
                   preferred_element_type=F32)


def _iota(shape, dim):
    return lax.broadcasted_iota(jnp.int32, shape, dim)


def _sigmoid(x):
    return 0.5 * (1.0 + jnp.tanh(0.5 * x))


def _silu(x):
    return x * _sigmoid(x)


def _chunk_cumsum(x, c):
    t = x.shape[0]
    blk = min(t, MXU_DEPTH)
    i = _iota((blk, blk), 0)
    j = _iota((blk, blk), 1)
    cum = jnp.where(((i // c) == (j // c)) & (j <= i), 1.0, 0.0).astype(BF16)
    return jnp.concatenate([_mm_exact_lhs(cum, x[b * blk:(b + 1) * blk], 3) for b in range(t // blk)], axis=0)


def _cparams(sem):
    return pltpu.CompilerParams(dimension_semantics=sem, vmem_limit_bytes=VMEM_LIMIT)


def _bd(x):
    h0 = _iota(x.shape, 1) < A_HEAD_DIM
    return jnp.concatenate([jnp.where(h0, x, 0.0), jnp.where(h0, 0.0, x)], axis=0)


def _unit_lower_inverse(ns):
    c = ns[0].shape[0]
    row = _iota(ns[0].shape, 0)
    col = _iota(ns[0].shape, 1) & (A_HEAD_DIM - 1)
    eye = jnp.where(col == row, 1.0, 0.0)
    first = (col == row - 1) & ((row & 1) == 1)
    ts = [eye + jnp.where(first, n, 0.0) for n in ns]
    bs = 2
    while bs < c:
        sel = ((row // (2 * bs)) == (col // (2 * bs))) & (((row // bs) & 1) == 1) & (((col // bs) & 1) == 0)
        xs = [_mm(jnp.where(sel, n, 0.0), _bd(t)) for n, t in zip(ns, ts)]
        ts = [t + _mm(t, _bd(x)) for t, x in zip(ts, xs)]
        bs *= 2
    return ts


def _rwkv_tile(r, k, v, lo, wcomb_ref, vec_ref, ga_sc, ya_sc,
               s_ref, v_sc, ad_sc, rd_sc, bi_sc, ki_sc, dc_sc, q_sc, y_sc, w_sc, bo_sc, ut_sc, vk_sc, par,
               side_work, n_side):
    tc = r.shape[0]
    c = CHUNK
    n_chunks = tc // c
    w0 = vec_ref[0:1, :]
    a0 = vec_ref[1:2, :]
    k_k = vec_ref[2:3, :]
    k_a = vec_ref[3:4, :]
    r_k = vec_ref[4:5, :]
    ln_w = vec_ref[5:6, :]
    ln_b = vec_ref[6:7, :]

    li = _iota((LANES, LANES), 0) // A_HEAD_DIM
    lj = _iota((LANES, LANES), 1) // A_HEAD_DIM
    bd_mask = li == lj
    head_ones = jnp.where(bd_mask, 1.0, 0.0).astype(BF16)
    head_ones2 = jnp.concatenate([head_ones, head_ones], axis=0)
    pairs = [slice(p * LANES, (p + 1) * LANES) for p in range(N_PAIRS)]

    def headsum(x):
        return jnp.concatenate([_mm_exact_rhs(x[:, sl], head_ones2, 2) for sl in pairs], axis=1)

    z = jnp.where(_iota(lo.shape, 1) < A_LORA, jnp.tanh(lo), lo)
    wa = _mm3(z, wcomb_ref[...])
    logw = -math.exp(-0.5) * _sigmoid(w0 + wa[:, 0:A_WIDTH])
    a = _sigmoid(a0 + wa[:, A_WIDTH:2 * A_WIDTH])
    kk = k * k_k
    kkn = kk * lax.rsqrt(jnp.maximum(headsum(kk * kk), 1e-24))
    k2 = k * (1.0 + (a - 1.0) * k_a)
    g = _chunk_cumsum(logw, c)
    e_neg = jnp.exp(-g)
    v_sc[...] = v
    ad_sc[...] = -kkn * jnp.exp(g - logw)
    rd_sc[...] = r * jnp.exp(g)
    bi_sc[...] = kkn * a * e_neg
    ki_sc[...] = k2 * e_neg
    for j in range(n_chunks):
        dc_sc[j * 8:(j + 1) * 8, :] = jnp.broadcast_to(jnp.exp(g[(j + 1) * c - 1:(j + 1) * c, :]), (8, A_WIDTH))
    bo_sc[...] = headsum(r * k2 * r_k) * v

    prow = _iota((c, LANES), 0)
    pcol = _iota((c, LANES), 1) & (A_HEAD_DIM - 1)
    strict = pcol < prow
    incl = pcol <= prow
    head0 = _iota((c, LANES), 1) < A_HEAD_DIM
    chains = [(j, p) for j in range(par) for p in range(N_PAIRS)]

    def local_body(gi, carry):
        def rows(j):
            return pl.ds(pl.multiple_of((gi * par + j) * c, c), c)

        a_dec = [ad_sc[rows(j), pairs[p]] for j, p in chains]
        r_dec = [rd_sc[rows(j), pairs[p]] for j, p in chains]
        b_inv = [bi_sc[rows(j), pairs[p]] for j, p in chains]
        k_inv = [ki_sc[rows(j), pairs[p]] for j, p in chains]
        vv = [v_sc[rows(j), pairs[p]] for j, p in chains]
        k_tail = [ki * dc_sc[pl.ds(pl.multiple_of((gi * par + j) * 8, 8), 1), pairs[p]]
                  for ki, (j, p) in zip(k_inv, chains)]

        gms = [_mm_nt(jnp.concatenate([ad, rd], axis=0),
                      jnp.concatenate([jnp.where(head0, bp, 0.0), jnp.where(head0, 0.0, bp),
                                       jnp.where(head0, kp, 0.0), jnp.where(head0, 0.0, kp)], axis=0))
               for ad, rd, bp, kp in zip(a_dec, r_dec, b_inv, k_inv)]
        n_ab = [jnp.where(strict, gm[0:c, 0:LANES], 0.0) for gm in gms]
        a_ak = [jnp.where(strict, gm[0:c, LANES:2 * LANES], 0.0) for gm in gms]
        a_rb = [jnp.where(incl, gm[c:2 * c, 0:LANES], 0.0) for gm in gms]
        a_rk = [jnp.where(incl, gm[c:2 * c, LANES:2 * LANES], 0.0) for gm in gms]
        tinv = _unit_lower_inverse(n_ab)
        from_v = [_mm(jnp.concatenate([ak, rk], axis=0), _bd(x)) for ak, rk, x in zip(a_ak, a_rk, vv)]
        tw = [_mm(t, jnp.concatenate([_bd(ad), _bd(fv[0:c])], axis=1))
              for t, fv, ad in zip(tinv, from_v, a_dec)]
        ab = [_mm(rb, jnp.concatenate([_bd(x[:, 0:LANES]), _bd(x[:, LANES:2 * LANES])], axis=1))
              for rb, x in zip(a_rb, tw)]
        vk = [_mm(x.T, kt) for x, kt in zip(vv, k_tail)]
        for i, (j, p) in enumerate(chains):
            slot = (gi * par + j) * N_PAIRS + p
            w_sc[rows(j), pairs[p]] = tw[i][:, 0:LANES]
            q_sc[rows(j), pairs[p]] = r_dec[i] + ab[i][:, 0:LANES]
            y_sc[rows(j), pairs[p]] = ab[i][:, LANES:2 * LANES] + from_v[i][c:2 * c]
            ut_sc[slot] = tw[i][:, LANES:2 * LANES].T
            vk_sc[slot] = jnp.where(bd_mask, vk[i], 0.0)
        return carry

    lax.fori_loop(0, n_chunks // par, local_body, 0)

    def state_step(ci):
        r0 = pl.multiple_of(ci * c, c)
        rows = pl.ds(r0, c)
        dc = dc_sc[pl.ds(pl.multiple_of(ci * 8, 8), 1), :]
        sts = [s_ref[p] for p in range(N_PAIRS)]
        uts = [_mm_nt(st, w_sc[rows, sl]) + ut_sc[ci * N_PAIRS + p] for p, (st, sl) in enumerate(zip(sts, pairs))]
        ys = [_mm_nt(q_sc[rows, sl], st) for st, sl in zip(sts, pairs)]
        upd = [_mm(ut, bi_sc[rows, sl] * dc[:, sl]) for ut, sl in zip(uts, pairs)]
        for p, sl in enumerate(pairs):
            s_ref[p] = sts[p] * dc[:, sl] + jnp.where(bd_mask, upd[p], 0.0) + vk_sc[ci * N_PAIRS + p]
            y_sc[rows, sl] = y_sc[rows, sl] + ys[p]

    steps = n_chunks // n_side

    def state_body(it, carry):
        for i in range(steps):
            state_step(it * steps + i)
        side_work(it)
        return carry

    lax.fori_loop(0, n_side, state_body, 0, unroll=2 if n_side % 2 == 0 else 1)

    y = y_sc[...]
    inv_n = 1.0 / A_HEAD_DIM
    mean = headsum(y) * inv_n
    d = y - mean
    var = headsum(d * d) * inv_n
    yn = d * lax.rsqrt(var + A_GN_EPS) * ln_w + ln_b
    ya_sc[...] = _bf((yn + bo_sc[...]) * _silu(ga_sc[...]))


def _swa_block_fn(first_tile, sink_ref, bias_ref, qb_sc, kv_sc, gb_sc, yb_sc):
    h0 = _iota((2 * BLOCK, B_KV_WIDTH), 1) < B_HEAD_DIM
    heads = [(hq // 2, hq % 2, hq // B_GROUP) for hq in range(B_HEADS)]

    def ext(straight, rolled, kvh, e):
        src = straight if kvh == e else rolled
        return _bf(jnp.where(h0, src, 0.0) if e == 0 else jnp.where(h0, 0.0, src))

    def block(j):
        r0 = pl.multiple_of(j * BLOCK, BLOCK)
        rows = pl.ds(r0, BLOCK)
        variant = jnp.where(first_tile & (j == 0), 0, 1)
        kv = kv_sc[pl.ds(r0, 2 * BLOCK), :]
        kc = kv[:, 0:B_KV_WIDTH]
        vc = kv[:, B_KV_WIDTH:2 * B_KV_WIDTH]
        kr = pltpu.roll(kc, B_HEAD_DIM, axis=1)
        vr = pltpu.roll(vc, B_HEAD_DIM, axis=1)
        kext = {(kvh, e): ext(kc, kr, kvh, e) for _, e, kvh in heads}
        vext = {(kvh, e): ext(vc, vr, kvh, e) for _, e, kvh in heads}
        q = [qb_sc[rows, p * LANES:(p + 1) * LANES] for p in range(B_WIDTH // LANES)]
        s = [lax.dot_general(q[p], kext[(kvh, e)], (((1,), (1,)), ((), ())), preferred_element_type=F32)
             + bias_ref[variant, hq] for hq, (p, e, kvh) in enumerate(heads)]
        m = [jnp.maximum(jnp.max(x, axis=-1, keepdims=True), sink_ref[hq]) for hq, x in enumerate(s)]
        pe = [jnp.exp(x - mx) for x, mx in zip(s, m)]
        inv = [1.0 / (jnp.sum(x, axis=-1, keepdims=True) + jnp.exp(sink_ref[hq] - mx))
               for hq, (x, mx) in enumerate(zip(pe, m))]
        pv = [jnp.dot(_bf(x), vext[(kvh, e)], preferred_element_type=F32) * rcp
              for x, rcp, (_, e, kvh) in zip(pe, inv, heads)]
        for p in range(B_WIDTH // LANES):
            sl = slice(p * LANES, (p + 1) * LANES)
            yb_sc[rows, sl] = _bf((pv[2 * p] + pv[2 * p + 1]) * _silu(gb_sc[rows, sl]))

    return block


def _layer0_kernel(sink_ref, x_ref, nw_ref, w_rkv, w_lora, w_ga, w_qb, w_kvb, w_gb, mu_rkv, mu_lora, mu_ga,
                   wcomb_ref, vec_ref, bias_ref, wo_a, wo_b, pnw_ref, o_ref,
                   c_rkv, c_lora, c_ga, ga_sc, qb_sc, kv_sc, gb_sc, ya_sc, yb_sc,
                   s_ref, v_sc, ad_sc, rd_sc, bi_sc, ki_sc, dc_sc, q_sc, y_sc, w_sc, bo_sc, ut_sc, vk_sc, *, par):
    ti = pl.program_id(1)
    tc = x_ref.shape[1]

    @pl.when(ti == 0)
    def _():
        c_rkv[...] = jnp.zeros_like(c_rkv)
        c_lora[...] = jnp.zeros_like(c_lora)
        c_ga[...] = jnp.zeros_like(c_ga)
        s_ref[...] = jnp.zeros_like(s_ref)
        kv_sc[0:BLOCK, :] = jnp.zeros((BLOCK, 2 * B_KV_WIDTH), F32)

    x = x_ref[0]
    h = x * lax.rsqrt(jnp.mean(x * x, axis=-1, keepdims=True) + NORM_EPS) * nw_ref[...]
    hb = _bf(h)

    def shifted(w, mu, carry):
        p = jnp.dot(hb, w[...], preferred_element_type=F32)
        prev = pltpu.roll(p, 1, axis=0)
        prev = jnp.where(_iota(p.shape, 0) == 0, carry[0:1, :], prev)
        carry[0:1, :] = p[tc - 1:tc, :]
        return p + mu[...] * (prev - p)

    rkv = shifted(w_rkv, mu_rkv, c_rkv)
    lo = shifted(w_lora, mu_lora, c_lora)
    ga_sc[...] = shifted(w_ga, mu_ga, c_ga)
    qb_sc[...] = _bf(jnp.dot(hb, w_qb[...], preferred_element_type=F32) * (B_HEAD_DIM ** -0.5))
    kv_sc[BLOCK:BLOCK + tc, :] = jnp.dot(hb, w_kvb[...], preferred_element_type=F32)
    gb_sc[...] = jnp.dot(hb, w_gb[...], preferred_element_type=F32)

    _rwkv_tile(rkv[:, 0:A_WIDTH], rkv[:, A_WIDTH:2 * A_WIDTH], rkv[:, 2 * A_WIDTH:3 * A_WIDTH], lo,
               wcomb_ref, vec_ref, ga_sc, ya_sc,
               s_ref, v_sc, ad_sc, rd_sc, bi_sc, ki_sc, dc_sc, q_sc, y_sc, w_sc, bo_sc, ut_sc, vk_sc, par,
               _swa_block_fn(ti == 0, sink_ref, bias_ref, qb_sc, kv_sc, gb_sc, yb_sc), tc // BLOCK)
    kv_sc[0:BLOCK, :] = kv_sc[tc:tc + BLOCK, :]

    y = (jnp.dot(ya_sc[...], wo_a[...], preferred_element_type=F32)
         + jnp.dot(yb_sc[...], wo_b[...], preferred_element_type=F32))
    yn = y * lax.rsqrt(jnp.mean(y * y, axis=-1, keepdims=True) + NORM_EPS) * pnw_ref[...]
    o_ref[0] = x_ref[0] + yn


def _layer0(sinks, x, nw, ws, mus, wcomb, vecs, bias, wo_a, wo_b, pnw, tc):
    bsz, t, d = x.shape
    n_chunks = tc // CHUNK
    par = math.gcd(A_PAR, n_chunks)
    full = lambda a: pl.BlockSpec(a.shape, lambda b, i: (0,) * a.ndim)
    tok = pl.BlockSpec((1, tc, d), lambda b, i: (b, i, 0))
    widths = [w.shape[1] for w in ws]
    tile = pltpu.VMEM((tc, A_WIDTH), F32)
    tile16 = pltpu.VMEM((tc, A_WIDTH), BF16)
    scratch = [pltpu.VMEM((8, widths[0]), F32), pltpu.VMEM((8, widths[1]), F32), pltpu.VMEM((8, widths[2]), F32),
               tile,
               tile16,
               pltpu.VMEM((tc + BLOCK, 2 * B_KV_WIDTH), F32),
               tile,
               tile16, tile16,
               pltpu.VMEM((N_PAIRS, LANES, LANES), F32),
               tile, tile, tile, tile, tile,
               pltpu.VMEM((n_chunks * 8, A_WIDTH), F32),
               tile, tile, tile, tile,
               pltpu.VMEM((n_chunks * N_PAIRS, LANES, CHUNK), F32),
               pltpu.VMEM((n_chunks * N_PAIRS, LANES, LANES), F32)]
    return pl.pallas_call(
        functools.partial(_layer0_kernel, par=par), grid=(bsz, t // tc),
        in_specs=[pl.BlockSpec(memory_space=pltpu.SMEM), tok, full(nw)] + [full(w) for w in ws]
                 + [full(m) for m in mus] + [full(wcomb), full(vecs), full(bias), full(wo_a), full(wo_b), full(pnw)],
        out_specs=tok, out_shape=jax.ShapeDtypeStruct((bsz, t, d), F32),
        scratch_shapes=scratch,
        compiler_params=_cparams(("parallel", "arbitrary")), name="layer0",
    )(sinks, x, nw, *ws, *mus, wcomb, vecs, bias, wo_a, wo_b, pnw)


def _layer1_kernel(x_ref, nw_ref, w_ref, lb_ref, cw_ref, wo_ref, pnw_ref, o_ref,
                   s_ref, qd_sc, ki_sc, v_sc, g_sc, dc_sc, o_sc, y_sc, *, par):
    ti = pl.program_id(1)

    @pl.when(ti == 0)
    def _():
        s_ref[...] = jnp.zeros_like(s_ref)

    tm = x_ref.shape[1]
    c = CHUNK
    x = x_ref[0]
    h = x * lax.rsqrt(jnp.mean(x * x, axis=-1, keepdims=True) + NORM_EPS) * nw_ref[...]
    p = jnp.dot(_bf(h), w_ref[...], preferred_element_type=F32)
    v_sc[...] = _bf(p[:, 2 * C_WIDTH:3 * C_WIDTH])
    g_sc[...] = p[:, 3 * C_WIDTH:4 * C_WIDTH]
    l0 = lb_ref[0:1, :]
    l1 = lb_ref[1:2, :]
    m = jnp.maximum(l0, l1)
    e0 = jnp.exp(l0 - m)
    e1 = jnp.exp(l1 - m)
    s0 = e0 / (e0 + e1)
    s1 = e1 / (e0 + e1)
    lb = (s0 + s1) - s0
    forget = lb + (1.0 - lb) * _sigmoid(p[:, C_WIDTH:2 * C_WIDTH])
    gc = _chunk_cumsum(jnp.log(forget), c)
    qd_sc[...] = _bf(_silu(p[:, 0:C_WIDTH]) * jnp.exp(gc))
    ki_sc[...] = (1.0 - forget) * jnp.exp(-gc)
    for j in range(tm // c):
        dc_sc[j * 8:(j + 1) * 8, :] = jnp.broadcast_to(jnp.exp(gc[(j + 1) * c - 1:(j + 1) * c, :]), (8, C_WIDTH))

    causal = _iota((c, c), 1) <= _iota((c, c), 0)
    heads = [slice(hd * C_DIM, (hd + 1) * C_DIM) for hd in range(C_HEADS)]
    chains = [(j, hd) for j in range(par) for hd in range(C_HEADS)]

    def chunk_body(gi, carry):
        def rows(j):
            return pl.ds(pl.multiple_of((gi * par + j) * c, c), c)

        q_dec = [qd_sc[rows(j), heads[hd]] for j, hd in chains]
        k_inv = [ki_sc[rows(j), heads[hd]] for j, hd in chains]
        vv = [v_sc[rows(j), heads[hd]] for j, hd in chains]
        dcs = [dc_sc[pl.ds(pl.multiple_of((gi * par + j) * 8, 8), 1), heads[hd]] for j, hd in chains]
        sc = [jnp.where(causal, _mm_nt(qd, ki), 0.0) for qd, ki in zip(q_dec, k_inv)]
        upd = [_mm(vx.astype(F32).T, ki * dc) for vx, ki, dc in zip(vv, k_inv, dcs)]
        intra = [_mm(s, vx) for s, vx in zip(sc, vv)]
        sts = [s_ref[hd] for hd in range(C_HEADS)]
        for i, (j, hd) in enumerate(chains):
            o_sc[rows(j), heads[hd]] = intra[i] + _mm_nt(q_dec[i], sts[hd])
            sts[hd] = sts[hd] * dcs[i] + upd[i]
        for hd in range(C_HEADS):
            s_ref[hd] = sts[hd]
        return carry

    lax.fori_loop(0, tm // (c * par), chunk_body, 0, unroll=True)

    cw = cw_ref[...]
    for sl in heads:
        o = o_sc[:, sl]
        on = o * lax.rsqrt(jnp.mean(o * o, axis=-1, keepdims=True) + NORM_EPS) * cw
        y_sc[:, sl] = _bf(on * _silu(g_sc[:, sl]))
    y = jnp.dot(y_sc[...], wo_ref[...], preferred_element_type=F32)
    yn = y * lax.rsqrt(jnp.mean(y * y, axis=-1, keepdims=True) + NORM_EPS) * pnw_ref[...]
    o_ref[0] = x_ref[0] + yn


def _layer1(x, nw, w, lbs, cw, wo, pnw, tm):
    bsz, t, d = x.shape
    par = math.gcd(C_PAR, tm // CHUNK)
    full = lambda a: pl.BlockSpec(a.shape, lambda b, i: (0,) * a.ndim)
    tok = pl.BlockSpec((1, tm, d), lambda b, i: (b, i, 0))
    act16 = pltpu.VMEM((tm, C_WIDTH), BF16)
    act32 = pltpu.VMEM((tm, C_WIDTH), F32)
    return pl.pallas_call(
        functools.partial(_layer1_kernel, par=par), grid=(bsz, t // tm),
        in_specs=[tok, full(nw), full(w), full(lbs), full(cw), full(wo), full(pnw)],
        out_specs=tok, out_shape=jax.ShapeDtypeStruct((bsz, t, d), F32),
        scratch_shapes=[pltpu.VMEM((C_HEADS, C_DIM, C_DIM), F32),
                        act16, act32, act16, act32,
                        pltpu.VMEM((tm // CHUNK * 8, C_WIDTH), F32),
                        act32, act16],
        compiler_params=_cparams(("parallel", "arbitrary")), name="layer1",
    )(x, nw, w, lbs, cw, wo, pnw)


def _t5_bucket(dist):
    max_exact = REL_BUCKETS // 2
    d = jnp.maximum(dist, 0)
    large = max_exact + (jnp.log(jnp.maximum(d, 1).astype(F32) / max_exact)
                         / math.log(REL_MAX_DIST / max_exact)
                         * (REL_BUCKETS - max_exact)).astype(jnp.int32)
    large = jnp.minimum(large, REL_BUCKETS - 1)
    return jnp.where(d < max_exact, d, large)


def _tile(t, want):
    return want if t % want == 0 else BLOCK


def kernel(x, rel_bias_table, lower_bounds, l0_pre_norm, l0_post_norm, l0_w_in, l0_w_out, a_mu, a_w0, a_w2, a_a0, a_a2, a_k_k, a_k_a, a_r_k, a_ln_w, a_ln_b, b_sinks, l1_pre_norm, l1_post_norm, l1_w_in, l1_w_out, c_norm_w):
    bsz, t, d = x.shape
    assert d == D_MODEL and t % BLOCK == 0
    row = lambda v: v.reshape(1, -1).astype(F32)

    c_rkv, c_lora, c_ga = 3 * A_WIDTH, 3 * A_WIDTH + 2 * A_LORA, 4 * A_WIDTH + 2 * A_LORA
    c_q, c_kv, c_gb = c_ga + B_WIDTH, c_ga + B_WIDTH + 2 * B_KV_WIDTH, c_ga + 2 * B_WIDTH + 2 * B_KV_WIDTH
    wb = _bf(l0_w_in)
    ws = [wb[:, 0:c_rkv], wb[:, c_rkv:c_lora], wb[:, c_lora:c_ga], wb[:, c_ga:c_q], wb[:, c_q:c_kv], wb[:, c_kv:c_gb]]
    mus = [row(a_mu[0:c_rkv]), row(a_mu[c_rkv:c_lora]), row(a_mu[c_lora:c_ga])]
    zeros = jnp.zeros((A_LORA, A_WIDTH), F32)
    wcomb = jnp.concatenate([jnp.concatenate([a_w2, zeros], axis=1), jnp.concatenate([zeros, a_a2], axis=1)], axis=0)
    vecs = jnp.stack([a_w0, a_a0, a_k_k, a_k_a, a_r_k.reshape(-1), a_ln_w, a_ln_b, jnp.zeros_like(a_w0)]).astype(F32)
    qi = jnp.arange(BLOCK)[:, None]
    kj = jnp.arange(2 * BLOCK)[None, :]
    dist = qi + BLOCK - kj
    onehot = (_t5_bucket(dist)[:, :, None] == jnp.arange(REL_BUCKETS)).astype(F32)
    bias = jnp.einsum("qkb,bh->hqk", onehot, rel_bias_table.astype(F32), precision=lax.Precision.HIGHEST)
    in_window = (dist >= 0) & (dist < WINDOW)
    bias = jnp.stack([jnp.where(in_window & (kj >= BLOCK), bias, NEG_INF), jnp.where(in_window, bias, NEG_INF)])
    w_out0 = _bf(l0_w_out)
    w_in1 = _bf(l1_w_in)
    w_out1 = _bf(l1_w_out)

    x1 = _layer0(b_sinks.astype(F32), x, row(l0_pre_norm), ws, mus, wcomb, vecs, bias,
                 w_out0[0:A_WIDTH], w_out0[A_WIDTH:], row(l0_post_norm), _tile(t, 512))
    return _layer1(x1, row(l1_pre_norm), w_in1, lower_bounds.astype(F32), row(c_norm_w), w_out1, row(l1_post_norm),
                   _tile(t, 256))
```

```python
import functools
import math

import jax
import jax.numpy as jnp
from jax import lax
from jax.experimental import pallas as pl
from jax.experimental.pallas import tpu as pltpu

D_MODEL = 1024
A_HEAD_DIM = 64
A_WIDTH = 512
A_LORA = 32
A_GN_EPS = 64e-5
B_HEAD_DIM = 64
B_WIDTH = 512
B_HEADS = 8
B_KV_WIDTH = 128
B_GROUP = 4
WINDOW = 128
BLOCK = 128
REL_BUCKETS = 32
REL_MAX_DIST = 128
C_HEADS = 8
C_DIM = 128
C_WIDTH = 1024
CHUNK = 64
NORM_EPS = 1e-6
NEG_INF = -1e30

LANES = 128
MXU_DEPTH = 256
N_PAIRS = A_WIDTH // LANES
A_PAR = 4
C_PAR = 2
VMEM_LIMIT = 56 * 1024 * 1024

F32 = jnp.float32
BF16 = jnp.bfloat16


def _bf(x):
    return x.astype(BF16)


def _mm(a, b):
    return jnp.dot(_bf(a), _bf(b), preferred_element_type=F32)


def _mm_nt(a, b):
    return lax.dot_general(_bf(a), _bf(b), (((1,), (1,)), ((), ())), preferred_element_type=F32)


def _split(x, parts):
    out = []
    rem = x
    for i in range(parts):
        hi = _bf(rem)
        out.append(hi)
        if i + 1 < parts:
            rem = rem - hi.astype(F32)
    return out


def _mm_exact_lhs(a_bf, x, parts):
    n = x.shape[1]
    xs = jnp.concatenate(_split(x, parts), axis=1)
    y = jnp.dot(a_bf, xs, preferred_element_type=F32)
    acc = y[:, 0:n]
    for i in range(1, parts):
        acc = acc + y[:, i * n:(i + 1) * n]
    return acc


def _mm_exact_rhs(x, b_stacked, parts):
    return jnp.dot(jnp.concatenate(_split(x, parts), axis=1), b_stacked, preferred_element_type=F32)


def _mm3(a, b):
    a_hi, a_lo = _split(a, 2)
    b_hi, b_lo = _split(b, 2)
    return jnp.dot(jnp.concatenate([a_hi, a_hi, a_lo], axis=1), jnp.concatenate([b_hi, b_lo, b_hi], axis=0),
                   preferred_element_type=F32)


def _iota(shape, dim):
    return lax.broadcasted_iota(jnp.int32, shape, dim)


def _sigmoid(x):
    return 0.5 * (1.0 + jnp.tanh(0.5 * x))


def _silu(x):
    return x * _sigmoid(x)


def _chunk_cumsum(x, c):
    t = x.shape[0]
    blk = min(t, MXU_DEPTH)
    i = _iota((blk, blk), 0)
    j = _iota((blk, blk), 1)
    cum = jnp.where(((i // c) == (j // c)) & (j <= i), 1.0, 0.0).astype(BF16)
    return jnp.concatenate([_mm_exact_lhs(cum, x[b * blk:(b + 1) * blk], 2) for b in range(t // blk)], axis=0)


def _cparams(sem):
    return pltpu.CompilerParams(dimension_semantics=sem, vmem_limit_bytes=VMEM_LIMIT)


def _bd(x):
    h0 = _iota(x.shape, 1) < A_HEAD_DIM
    return jnp.concatenate([jnp.where(h0, x, 0.0), jnp.where(h0, 0.0, x)], axis=0)


def _unit_lower_inverse(ns):
    c = ns[0].shape[0]
    row = _iota(ns[0].shape, 0)
    col = _iota(ns[0].shape, 1) & (A_HEAD_DIM - 1)
    eye = jnp.where(col == row, 1.0, 0.0)
    first = (col == row - 1) & ((row & 1) == 1)
    ts = [eye + jnp.where(first, n, 0.0) for n in ns]
    bs = 2
    while bs < c:
        sel = ((row // (2 * bs)) == (col // (2 * bs))) & (((row // bs) & 1) == 1) & (((col // bs) & 1) == 0)
        xs = [_mm(jnp.where(sel, n, 0.0), _bd(t)) for n, t in zip(ns, ts)]
        ts = [t + _mm(t, _bd(x)) for t, x in zip(ts, xs)]
        bs *= 2
    return ts


def _rwkv_tile(r, k, v, lo, wcomb_ref, vec_ref, ga_sc, ya_sc,
               s_ref, v_sc, ad_sc, rd_sc, bi_sc, ki_sc, dc_sc, q_sc, y_sc, w_sc, bo_sc, ut_sc, vk_sc, par,
               side_work, n_side):
    tc = r.shape[0]
    c = CHUNK
    n_chunks = tc // c
    w0 = vec_ref[0:1, :]
    a0 = vec_ref[1:2, :]
    k_k = vec_ref[2:3, :]
    k_a = vec_ref[3:4, :]
    r_k = vec_ref[4:5, :]
    ln_w = vec_ref[5:6, :]
    ln_b = vec_ref[6:7, :]

    li = _iota((LANES, LANES), 0) // A_HEAD_DIM
    lj = _iota((LANES, LANES), 1) // A_HEAD_DIM
    bd_mask = li == lj
    head_ones = jnp.where(bd_mask, 1.0, 0.0).astype(BF16)
    head_ones2 = jnp.concatenate([head_ones, head_ones], axis=0)
    pairs = [slice(p * LANES, (p + 1) * LANES) for p in range(N_PAIRS)]

    def headsum(x):
        return jnp.concatenate([_mm_exact_rhs(x[:, sl], head_ones2, 2) for sl in pairs], axis=1)

    z = jnp.where(_iota(lo.shape, 1) < A_LORA, jnp.tanh(lo), lo)
    wa = _mm3(z, wcomb_ref[...])
    logw = -math.exp(-0.5) * _sigmoid(w0 + wa[:, 0:A_WIDTH])
    a = _sigmoid(a0 + wa[:, A_WIDTH:2 * A_WIDTH])
    kk = k * k_k
    kkn = kk * lax.rsqrt(jnp.maximum(headsum(kk * kk), 1e-24))
    k2 = k * (1.0 + (a - 1.0) * k_a)
    g = _chunk_cumsum(logw, c)
    e_neg = jnp.exp(-g)
    v_sc[...] = v
    ad_sc[...] = -kkn * jnp.exp(g - logw)
    rd_sc[...] = r * jnp.exp(g)
    bi_sc[...] = kkn * a * e_neg
    ki_sc[...] = k2 * e_neg
    for j in range(n_chunks):
        dc_sc[j * 8:(j + 1) * 8, :] = jnp.broadcast_to(jnp.exp(g[(j + 1) * c - 1:(j + 1) * c, :]), (8, A_WIDTH))
    bo_sc[...] = headsum(r * k2 * r_k) * v

    prow = _iota((c, LANES), 0)
    pcol = _iota((c, LANES), 1) & (A_HEAD_DIM - 1)
    strict = pcol < prow
    incl = pcol <= prow
    head0 = _iota((c, LANES), 1) < A_HEAD_DIM
    chains = [(j, p) for j in range(par) for p in range(N_PAIRS)]

    def local_body(gi, carry):
        def rows(j):
            return pl.ds(pl.multiple_of((gi * par + j) * c, c), c)

        a_dec = [ad_sc[rows(j), pairs[p]] for j, p in chains]
        r_dec = [rd_sc[rows(j), pairs[p]] for j, p in chains]
        b_inv = [bi_sc[rows(j), pairs[p]] for j, p in chains]
        k_inv = [ki_sc[rows(j), pairs[p]] for j, p in chains]
        vv = [v_sc[rows(j), pairs[p]] for j, p in chains]
        k_tail = [ki * dc_sc[pl.ds(pl.multiple_of((gi * par + j) * 8, 8), 1), pairs[p]]
                  for ki, (j, p) in zip(k_inv, chains)]

        gms = [_mm_nt(jnp.concatenate([ad, rd], axis=0),
                      jnp.concatenate([jnp.where(head0, bp, 0.0), jnp.where(head0, 0.0, bp),
                                       jnp.where(head0, kp, 0.0), jnp.where(head0, 0.0, kp)], axis=0))
               for ad, rd, bp, kp in zip(a_dec, r_dec, b_inv, k_inv)]
        n_ab = [jnp.where(strict, gm[0:c, 0:LANES], 0.0) for gm in gms]
        a_ak = [jnp.where(strict, gm[0:c, LANES:2 * LANES], 0.0) for gm in gms]
        a_rb = [jnp.where(incl, gm[c:2 * c, 0:LANES], 0.0) for gm in gms]
        a_rk = [jnp.where(incl, gm[c:2 * c, LANES:2 * LANES], 0.0) for gm in gms]
        tinv = _unit_lower_inverse(n_ab)
        from_v = [_mm(jnp.concatenate([ak, rk], axis=0), _bd(x)) for ak, rk, x in zip(a_ak, a_rk, vv)]
        tw = [_mm(t, jnp.concatenate([_bd(ad), _bd(fv[0:c])], axis=1))
              for t, fv, ad in zip(tinv, from_v, a_dec)]
        ab = [_mm(rb, jnp.concatenate([_bd(x[:, 0:LANES]), _bd(x[:, LANES:2 * LANES])], axis=1))
              for rb, x in zip(a_rb, tw)]
        vk = [_mm(x.T, kt) for x, kt in zip(vv, k_tail)]
        for i, (j, p) in enumerate(chains):
            slot = (gi * par + j) * N_PAIRS + p
            w_sc[rows(j), pairs[p]] = tw[i][:, 0:LANES]
            q_sc[rows(j), pairs[p]] = r_dec[i] + ab[i][:, 0:LANES]
            y_sc[rows(j), pairs[p]] = ab[i][:, LANES:2 * LANES] + from_v[i][c:2 * c]
            ut_sc[slot] = tw[i][:, LANES:2 * LANES].T
            vk_sc[slot] = jnp.where(bd_mask, vk[i], 0.0)
        return carry

    lax.fori_loop(0, n_chunks // par, local_body, 0, unroll=True)

    def state_step(ci):
        r0 = pl.multiple_of(ci * c, c)
        rows = pl.ds(r0, c)
        dc = dc_sc[pl.ds(pl.multiple_of(ci * 8, 8), 1), :]
        sts = [s_ref[p] for p in range(N_PAIRS)]
        uts = [_mm_nt(st, w_sc[rows, sl]) + ut_sc[ci * N_PAIRS + p] for p, (st, sl) in enumerate(zip(sts, pairs))]
        ys = [_mm_nt(q_sc[rows, sl], st) for st, sl in zip(sts, pairs)]
        upd = [_mm(ut, bi_sc[rows, sl] * dc[:, sl]) for ut, sl in zip(uts, pairs)]
        for p, sl in enumerate(pairs):
            s_ref[p] = sts[p] * dc[:, sl] + jnp.where(bd_mask, upd[p], 0.0) + vk_sc[ci * N_PAIRS + p]
            y_sc[rows, sl] = y_sc[rows, sl] + ys[p]

    steps = n_chunks // n_side

    def state_body(it, carry):
        for i in range(steps):
            state_step(it * steps + i)
        side_work(it)
        return carry

    lax.fori_loop(0, n_side, state_body, 0, unroll=True)

    y = y_sc[...]
    inv_n = 1.0 / A_HEAD_DIM
    mean = headsum(y) * inv_n
    d = y - mean
    var = headsum(d * d) * inv_n
    yn = d * lax.rsqrt(var + A_GN_EPS) * ln_w + ln_b
    ya_sc[...] = _bf((yn + bo_sc[...]) * _silu(ga_sc[...]))


def _swa_block_fn(first_tile, sink_ref, bias_ref, qb_sc, kv_sc, gb_sc, yb_sc):
    h0 = _iota((2 * BLOCK, B_KV_WIDTH), 1) < B_HEAD_DIM
    heads = [(hq // 2, hq % 2, hq // B_GROUP) for hq in range(B_HEADS)]

    def ext(straight, rolled, kvh, e):
        src = straight if kvh == e else rolled
        return _bf(jnp.where(h0, src, 0.0) if e == 0 else jnp.where(h0, 0.0, src))

    def block(j):
        r0 = pl.multiple_of(j * BLOCK, BLOCK)
        rows = pl.ds(r0, BLOCK)
        variant = jnp.where(first_tile & (j == 0), 0, 1)
        kv = kv_sc[pl.ds(r0, 2 * BLOCK), :]
        kc = kv[:, 0:B_KV_WIDTH]
        vc = kv[:, B_KV_WIDTH:2 * B_KV_WIDTH]
        kr = pltpu.roll(kc, B_HEAD_DIM, axis=1)
        vr = pltpu.roll(vc, B_HEAD_DIM, axis=1)
        kext = {(kvh, e): ext(kc, kr, kvh, e) for _, e, kvh in heads}
        vext = {(kvh, e): ext(vc, vr, kvh, e) for _, e, kvh in heads}
        q = [qb_sc[rows, p * LANES:(p + 1) * LANES] for p in range(B_WIDTH // LANES)]
        s = [lax.dot_general(q[p], kext[(kvh, e)], (((1,), (1,)), ((), ())), preferred_element_type=F32)
             + bias_ref[variant, hq] for hq, (p, e, kvh) in enumerate(heads)]
        m = [jnp.maximum(jnp.max(x, axis=-1, keepdims=True), sink_ref[hq]) for hq, x in enumerate(s)]
        pe = [jnp.exp(x - mx) for x, mx in zip(s, m)]
        inv = [1.0 / (jnp.sum(x, axis=-1, keepdims=True) + jnp.exp(sink_ref[hq] - mx))
               for hq, (x, mx) in enumerate(zip(pe, m))]
        pv = [jnp.dot(_bf(x), vext[(kvh, e)], preferred_element_type=F32) * rcp
              for x, rcp, (_, e, kvh) in zip(pe, inv, heads)]
        for p in range(B_WIDTH // LANES):
            sl = slice(p * LANES, (p + 1) * LANES)
            yb_sc[rows, sl] = _bf((pv[2 * p] + pv[2 * p + 1]) * _silu(gb_sc[rows, sl]))

    return block


def _layer0_kernel(sink_ref, x_ref, nw_ref, w_rkv, w_lora, w_ga, w_qb, w_kvb, w_gb, mu_rkv, mu_lora, mu_ga,
                   wcomb_ref, vec_ref, bias_ref, wo_a, wo_b, pnw_ref, o_ref,
                   c_rkv, c_lora, c_ga, ga_sc, qb_sc, kv_sc, gb_sc, ya_sc, yb_sc,
                   s_ref, v_sc, ad_sc, rd_sc, bi_sc, ki_sc, dc_sc, q_sc, y_sc, w_sc, bo_sc, ut_sc, vk_sc, *, par):
    ti = pl.program_id(1)
    tc = x_ref.shape[1]

    @pl.when(ti == 0)
    def _():
        c_rkv[...] = jnp.zeros_like(c_rkv)
        c_lora[...] = jnp.zeros_like(c_lora)
        c_ga[...] = jnp.zeros_like(c_ga)
        s_ref[...] = jnp.zeros_like(s_ref)
        kv_sc[0:BLOCK, :] = jnp.zeros((BLOCK, 2 * B_KV_WIDTH), F32)

    x = x_ref[0]
    h = x * lax.rsqrt(jnp.mean(x * x, axis=-1, keepdims=True) + NORM_EPS) * nw_ref[...]
    hb = _bf(h)

    def shifted(w, mu, carry):
        p = jnp.dot(hb, w[...], preferred_element_type=F32)
        prev = pltpu.roll(p, 1, axis=0)
        prev = jnp.where(_iota(p.shape, 0) == 0, carry[0:1, :], prev)
        carry[0:1, :] = p[tc - 1:tc, :]
        return p + mu[...] * (prev - p)

    rkv = shifted(w_rkv, mu_rkv, c_rkv)
    lo = shifted(w_lora, mu_lora, c_lora)
    ga_sc[...] = shifted(w_ga, mu_ga, c_ga)
    qb_sc[...] = _bf(jnp.dot(hb, w_qb[...], preferred_element_type=F32) * (B_HEAD_DIM ** -0.5))
    kv_sc[BLOCK:BLOCK + tc, :] = jnp.dot(hb, w_kvb[...], preferred_element_type=F32)
    gb_sc[...] = jnp.dot(hb, w_gb[...], preferred_element_type=F32)

    _rwkv_tile(rkv[:, 0:A_WIDTH], rkv[:, A_WIDTH:2 * A_WIDTH], rkv[:, 2 * A_WIDTH:3 * A_WIDTH], lo,
               wcomb_ref, vec_ref, ga_sc, ya_sc,
               s_ref, v_sc, ad_sc, rd_sc, bi_sc, ki_sc, dc_sc, q_sc, y_sc, w_sc, bo_sc, ut_sc, vk_sc, par,
               _swa_block_fn(ti == 0, sink_ref, bias_ref, qb_sc, kv_sc, gb_sc, yb_sc), tc // BLOCK)
    kv_sc[0:BLOCK, :] = kv_sc[tc:tc + BLOCK, :]

    y = (jnp.dot(ya_sc[...], wo_a[...], preferred_element_type=F32)
         + jnp.dot(yb_sc[...], wo_b[...], preferred_element_type=F32))
    yn = y * lax.rsqrt(jnp.mean(y * y, axis=-1, keepdims=True) + NORM_EPS) * pnw_ref[...]
    o_ref[0] = x_ref[0] + yn


def _layer0(sinks, x, nw, ws, mus, wcomb, vecs, bias, wo_a, wo_b, pnw, tc):
    bsz, t, d = x.shape
    n_chunks = tc // CHUNK
    par = math.gcd(A_PAR, n_chunks)
    full = lambda a: pl.BlockSpec(a.shape, lambda b, i: (0,) * a.ndim)
    tok = pl.BlockSpec((1, tc, d), lambda b, i: (b, i, 0))
    widths = [w.shape[1] for w in ws]
    tile = pltpu.VMEM((tc, A_WIDTH), F32)
    tile16 = pltpu.VMEM((tc, A_WIDTH), BF16)
    scratch = [pltpu.VMEM((8, widths[0]), F32), pltpu.VMEM((8, widths[1]), F32), pltpu.VMEM((8, widths[2]), F32),
               tile,
               tile16,
               pltpu.VMEM((tc + BLOCK, 2 * B_KV_WIDTH), F32),
               tile,
               tile16, tile16,
               pltpu.VMEM((N_PAIRS, LANES, LANES), F32),
               tile, tile, tile, tile, tile,
               pltpu.VMEM((n_chunks * 8, A_WIDTH), F32),
               tile, tile, tile, tile,
               pltpu.VMEM((n_chunks * N_PAIRS, LANES, CHUNK), F32),
               pltpu.VMEM((n_chunks * N_PAIRS, LANES, LANES), F32)]
    return pl.pallas_call(
        functools.partial(_layer0_kernel, par=par), grid=(bsz, t // tc),
        in_specs=[pl.BlockSpec(memory_space=pltpu.SMEM), tok, full(nw)] + [full(w) for w in ws]
                 + [full(m) for m in mus] + [full(wcomb), full(vecs), full(bias), full(wo_a), full(wo_b), full(pnw)],
        out_specs=tok, out_shape=jax.ShapeDtypeStruct((bsz, t, d), F32),
        scratch_shapes=scratch,
        compiler_params=_cparams(("parallel", "arbitrary")), name="layer0",
    )(sinks, x, nw, *ws, *mus, wcomb, vecs, bias, wo_a, wo_b, pnw)


def _layer1_kernel(x_ref, nw_ref, w_ref, lb_ref, cw_ref, wo_ref, pnw_ref, o_ref,
                   s_ref, qd_sc, ki_sc, v_sc, g_sc, dc_sc, o_sc, y_sc, *, par):
    ti = pl.program_id(1)

    @pl.when(ti == 0)
    def _():
        s_ref[...] = jnp.zeros_like(s_ref)

    tm = x_ref.shape[1]
    c = CHUNK
    x = x_ref[0]
    h = x * lax.rsqrt(jnp.mean(x * x, axis=-1, keepdims=True) + NORM_EPS) * nw_ref[...]
    p = jnp.dot(_bf(h), w_ref[...], preferred_element_type=F32)
    v_sc[...] = _bf(p[:, 2 * C_WIDTH:3 * C_WIDTH])
    g_sc[...] = p[:, 3 * C_WIDTH:4 * C_WIDTH]
    l0 = lb_ref[0:1, :]
    l1 = lb_ref[1:2, :]
    m = jnp.maximum(l0, l1)
    e0 = jnp.exp(l0 - m)
    e1 = jnp.exp(l1 - m)
    s0 = e0 / (e0 + e1)
    s1 = e1 / (e0 + e1)
    lb = (s0 + s1) - s0
    forget = lb + (1.0 - lb) * _sigmoid(p[:, C_WIDTH:2 * C_WIDTH])
    gc = _chunk_cumsum(jnp.log(forget), c)
    qd_sc[...] = _bf(_silu(p[:, 0:C_WIDTH]) * jnp.exp(gc))
    ki_sc[...] = (1.0 - forget) * jnp.exp(-gc)
    for j in range(tm // c):
        dc_sc[j * 8:(j + 1) * 8, :] = jnp.broadcast_to(jnp.exp(gc[(j + 1) * c - 1:(j + 1) * c, :]), (8, C_WIDTH))

    causal = _iota((c, c), 1) <= _iota((c, c), 0)
    heads = [slice(hd * C_DIM, (hd + 1) * C_DIM) for hd in range(C_HEADS)]
    chains = [(j, hd) for j in range(par) for hd in range(C_HEADS)]

    def chunk_body(gi, carry):
        def rows(j):
            return pl.ds(pl.multiple_of((gi * par + j) * c, c), c)

        q_dec = [qd_sc[rows(j), heads[hd]] for j, hd in chains]
        k_inv = [ki_sc[rows(j), heads[hd]] for j, hd in chains]
        vv = [v_sc[rows(j), heads[hd]] for j, hd in chains]
        dcs = [dc_sc[pl.ds(pl.multiple_of((gi * par + j) * 8, 8), 1), heads[hd]] for j, hd in chains]
        sc = [jnp.where(causal, _mm_nt(qd, ki), 0.0) for qd, ki in zip(q_dec, k_inv)]
        upd = [_mm(vx.astype(F32).T, ki * dc) for vx, ki, dc in zip(vv, k_inv, dcs)]
        intra = [_mm(s, vx) for s, vx in zip(sc, vv)]
        sts = [s_ref[hd] for hd in range(C_HEADS)]
        for i, (j, hd) in enumerate(chains):
            o_sc[rows(j), heads[hd]] = intra[i] + _mm_nt(q_dec[i], sts[hd])
            sts[hd] = sts[hd] * dcs[i] + upd[i]
        for hd in range(C_HEADS):
            s_ref[hd] = sts[hd]
        return carry

    lax.fori_loop(0, tm // (c * par), chunk_body, 0, unroll=True)

    cw = cw_ref[...]
    for sl in heads:
        o = o_sc[:, sl]
        on = o * lax.rsqrt(jnp.mean(o * o, axis=-1, keepdims=True) + NORM_EPS) * cw
        y_sc[:, sl] = _bf(on * _silu(g_sc[:, sl]))
    y = jnp.dot(y_sc[...], wo_ref[...], preferred_element_type=F32)
    yn = y * lax.rsqrt(jnp.mean(y * y, axis=-1, keepdims=True) + NORM_EPS) * pnw_ref[...]
    o_ref[0] = x_ref[0] + yn


def _layer1(x, nw, w, lbs, cw, wo, pnw, tm):
    bsz, t, d = x.shape
    par = math.gcd(C_PAR, tm // CHUNK)
    full = lambda a: pl.BlockSpec(a.shape, lambda b, i: (0,) * a.ndim)
    tok = pl.BlockSpec((1, tm, d), lambda b, i: (b, i, 0))
    act16 = pltpu.VMEM((tm, C_WIDTH), BF16)
    act32 = pltpu.VMEM((tm, C_WIDTH), F32)
    return pl.pallas_call(
        functools.partial(_layer1_kernel, par=par), grid=(bsz, t // tm),
        in_specs=[tok, full(nw), full(w), full(lbs), full(cw), full(wo), full(pnw)],
        out_specs=tok, out_shape=jax.ShapeDtypeStruct((bsz, t, d), F32),
        scratch_shapes=[pltpu.VMEM((C_HEADS, C_DIM, C_DIM), F32),
                        act16, act32, act16, act32,
                        pltpu.VMEM((tm // CHUNK * 8, C_WIDTH), F32),
                        act32, act16],
        compiler_params=_cparams(("parallel", "arbitrary")), name="layer1",
    )(x, nw, w, lbs, cw, wo, pnw)


def _t5_bucket(dist):
    max_exact = REL_BUCKETS // 2
    d = jnp.maximum(dist, 0)
    large = max_exact + (jnp.log(jnp.maximum(d, 1).astype(F32) / max_exact)
                         / math.log(REL_MAX_DIST / max_exact)
                         * (REL_BUCKETS - max_exact)).astype(jnp.int32)
    large = jnp.minimum(large, REL_BUCKETS - 1)
    return jnp.where(d < max_exact, d, large)


def _tile(t, want):
    return want if t % want == 0 else BLOCK


def kernel(x, rel_bias_table, lower_bounds, l0_pre_norm, l0_post_norm, l0_w_in, l0_w_out, a_mu, a_w0, a_w2, a_a0, a_a2, a_k_k, a_k_a, a_r_k, a_ln_w, a_ln_b, b_sinks, l1_pre_norm, l1_post_norm, l1_w_in, l1_w_out, c_norm_w):
    bsz, t, d = x.shape
    assert d == D_MODEL and t % BLOCK == 0
    row = lambda v: v.reshape(1, -1).astype(F32)

    c_rkv, c_lora, c_ga = 3 * A_WIDTH, 3 * A_WIDTH + 2 * A_LORA, 4 * A_WIDTH + 2 * A_LORA
    c_q, c_kv, c_gb = c_ga + B_WIDTH, c_ga + B_WIDTH + 2 * B_KV_WIDTH, c_ga + 2 * B_WIDTH + 2 * B_KV_WIDTH
    wb = _bf(l0_w_in)
    ws = [wb[:, 0:c_rkv], wb[:, c_rkv:c_lora], wb[:, c_lora:c_ga], wb[:, c_ga:c_q], wb[:, c_q:c_kv], wb[:, c_kv:c_gb]]
    mus = [row(a_mu[0:c_rkv]), row(a_mu[c_rkv:c_lora]), row(a_mu[c_lora:c_ga])]
    zeros = jnp.zeros((A_LORA, A_WIDTH), F32)
    wcomb = jnp.concatenate([jnp.concatenate([a_w2, zeros], axis=1), jnp.concatenate([zeros, a_a2], axis=1)], axis=0)
    vecs = jnp.stack([a_w0, a_a0, a_k_k, a_k_a, a_r_k.reshape(-1), a_ln_w, a_ln_b, jnp.zeros_like(a_w0)]).astype(F32)
    qi = jnp.arange(BLOCK)[:, None]
    kj = jnp.arange(2 * BLOCK)[None, :]
    dist = qi + BLOCK - kj
    onehot = (_t5_bucket(dist)[:, :, None] == jnp.arange(REL_BUCKETS)).astype(F32)
    bias = jnp.einsum("qkb,bh->hqk", onehot, rel_bias_table.astype(F32), precision=lax.Precision.HIGHEST)
    in_window = (dist >= 0) & (dist < WINDOW)
    bias = jnp.stack([jnp.where(in_window & (kj >= BLOCK), bias, NEG_INF), jnp.where(in_window, bias, NEG_INF)])
    w_out0 = _bf(l0_w_out)
    w_in1 = _bf(l1_w_in)
    w_out1 = _bf(l1_w_out)

    x1 = _layer0(b_sinks.astype(F32), x, row(l0_pre_norm), ws, mus, wcomb, vecs, bias,
                 w_out0[0:A_WIDTH], w_out0[A_WIDTH:], row(l0_post_norm), _tile(t, 512))
    return _layer1(x1, row(l1_pre_norm), w_in1, lower_bounds.astype(F32), row(c_norm_w), w_out1, row(l1_post_norm),
                   _tile(t, 512))
```

```python
import functools
import math

import jax
import jax.numpy as jnp
from jax import lax
from jax.experimental import pallas as pl
from jax.experimental.pallas import tpu as pltpu

D_MODEL = 1024
A_HEAD_DIM = 64
A_WIDTH = 512
A_LORA = 32
A_GN_EPS = 64e-5
B_HEAD_DIM = 64
B_WIDTH = 512
B_HEADS = 8
B_KV_WIDTH = 128
B_GROUP = 4
WINDOW = 128
BLOCK = 128
REL_BUCKETS = 32
REL_MAX_DIST = 128
C_HEADS = 8
C_DIM = 128
C_WIDTH = 1024
CHUNK = 64
NORM_EPS = 1e-6
NEG_INF = -1e30
LOG2E = math.log2(math.e)

LANES = 128
MXU_DEPTH = 256
N_PAIRS = A_WIDTH // LANES
A_PAR = 4
C_PAR = 2
C_GROUP = 4
VMEM_LIMIT = 56 * 1024 * 1024

F32 = jnp.float32
BF16 = jnp.bfloat16


def _bf(x):
    return x.astype(BF16)


def _mm(a, b):
    return jnp.dot(_bf(a), _bf(b), preferred_element_type=F32)


def _mm_nt(a, b):
    return lax.dot_general(_bf(a), _bf(b), (((1,), (1,)), ((), ())), preferred_element_type=F32)


def _split(x, parts):
    out = []
    rem = x
    for i in range(parts):
        hi = _bf(rem)
        out.append(hi)
        if i + 1 < parts:
            rem = rem - hi.astype(F32)
    return out


def _mm_exact_lhs(a_bf, x, parts):
    n = x.shape[1]
    xs = jnp.concatenate(_split(x, parts), axis=1)
    y = jnp.dot(a_bf, xs, preferred_element_type=F32)
    acc = y[:, 0:n]
    for i in range(1, parts):
        acc = acc + y[:, i * n:(i + 1) * n]
    return acc


def _mm_exact_rhs(x, b_stacked, parts):
    return jnp.dot(jnp.concatenate(_split(x, parts), axis=1), b_stacked, preferred_element_type=F32)


def _mm3(a, b):
    a_hi, a_lo = _split(a, 2)
    b_hi, b_lo = _split(b, 2)
    return jnp.dot(jnp.concatenate([a_hi, a_hi, a_lo], axis=1), jnp.concatenate([b_hi, b_lo, b_hi], axis=0),
                   preferred_element_type=F32)


def _iota(shape, dim):
    return lax.broadcasted_iota(jnp.int32, shape, dim)


def _sigmoid(x):
    return 0.5 * (1.0 + jnp.tanh(0.5 * x))


def _silu(x):
    return x * _sigmoid(x)


def _chunk_cumsum(x, c):
    t = x.shape[0]
    blk = min(t, MXU_DEPTH)
    i = _iota((blk, blk), 0)
    j = _iota((blk, blk), 1)
    cum = jnp.where(((i // c) == (j // c)) & (j <= i), 1.0, 0.0).astype(BF16)
    return jnp.concatenate([_mm_exact_lhs(cum, x[b * blk:(b + 1) * blk], 2) for b in range(t // blk)], axis=0)


def _rms(x, w):
    return x * lax.rsqrt(jnp.mean(x * x, axis=-1, keepdims=True) + NORM_EPS) * w


def _cparams(sem):
    return pltpu.CompilerParams(dimension_semantics=sem, vmem_limit_bytes=VMEM_LIMIT)


def _chain(*gens):
    for gen in gens:
        yield from gen


def _run_interleaved(*stages):
    live = [[gen, max(n, 1), 0] for gen, n in stages if gen is not None]
    ticks = max((n for _, n, _ in live), default=0)
    for t in range(ticks + 1):
        for item in live:
            gen, n, done = item
            want = n if t == ticks else (t + 1) * n // ticks
            while gen is not None and (done < want or t == ticks):
                try:
                    next(gen)
                    done += 1
                except StopIteration:
                    gen = None
            item[0], item[2] = gen, done


def _bd(x):
    h0 = _iota(x.shape, 1) < A_HEAD_DIM
    return jnp.concatenate([jnp.where(h0, x, 0.0), jnp.where(h0, 0.0, x)], axis=0)


def _unit_lower_inverse(ns, out):
    c = ns[0].shape[0]
    row = _iota(ns[0].shape, 0)
    col = _iota(ns[0].shape, 1) & (A_HEAD_DIM - 1)
    eye = jnp.where(col == row, 1.0, 0.0)
    first = (col == row - 1) & ((row & 1) == 1)
    ts = [eye + jnp.where(first, n, 0.0) for n in ns]
    bs = 2
    while bs < c:
        sel = ((row // (2 * bs)) == (col // (2 * bs))) & (((row // bs) & 1) == 1) & (((col // bs) & 1) == 0)
        xs = [_mm(jnp.where(sel, n, 0.0), _bd(t)) for n, t in zip(ns, ts)]
        yield
        ts = [t + _mm(t, _bd(x)) for t, x in zip(ts, xs)]
        yield
        bs *= 2
    out[:] = ts


class _RwkvScratch:
    def __init__(self, s, v, ad, rd, bi, ki, dc, q, y, w, bo, uv):
        self.s, self.v, self.ad, self.rd, self.bi, self.ki, self.dc = s, v, ad, rd, bi, ki, dc
        self.q, self.y, self.w, self.bo, self.uv = q, y, w, bo, uv


def _head_pair_consts():
    li = _iota((LANES, LANES), 0) // A_HEAD_DIM
    lj = _iota((LANES, LANES), 1) // A_HEAD_DIM
    bd_mask = li == lj
    head_ones = jnp.where(bd_mask, 1.0, 0.0).astype(BF16)
    return bd_mask, jnp.concatenate([head_ones, head_ones], axis=0)


_PAIRS = [slice(p * LANES, (p + 1) * LANES) for p in range(N_PAIRS)]


def _headsum(x, head_ones2):
    return jnp.concatenate([_mm_exact_rhs(x[:, sl], head_ones2, 2) for sl in _PAIRS], axis=1)


def _rwkv_prep(r0, proj, wcomb_ref, vec_ref, sc, head_ones2):
    r, k, v, lo = proj["r"], proj["k"], proj["v"], proj["lo"]
    n = r.shape[0]
    c = CHUNK
    rows = slice(r0, r0 + n)
    w0, a0, k_k, k_a, r_k = (vec_ref[i:i + 1, :] for i in range(5))
    z = jnp.where(_iota(lo.shape, 1) < A_LORA, jnp.tanh(lo), lo)
    wa = _mm3(z, wcomb_ref[...])
    logw = -math.exp(-0.5) * _sigmoid(w0 + wa[:, 0:A_WIDTH])
    yield
    g = _chunk_cumsum(logw, c)
    a = _sigmoid(a0 + wa[:, A_WIDTH:2 * A_WIDTH])
    yield
    kk = k * k_k
    kkn = kk * lax.rsqrt(jnp.maximum(_headsum(kk * kk, head_ones2), 1e-24))
    yield
    k2 = k * (1.0 + (a - 1.0) * k_a)
    sc.v[rows, :] = v
    sc.bo[rows, :] = _headsum(r * k2 * r_k, head_ones2) * v
    yield
    e_neg = jnp.exp(-g)
    sc.bi[rows, :] = kkn * a * e_neg
    sc.ki[rows, :] = k2 * e_neg
    yield
    sc.ad[rows, :] = -kkn * jnp.exp(g - logw)
    sc.rd[rows, :] = r * jnp.exp(g)
    for j in range(n // c):
        slab = (r0 // c + j) * 8
        sc.dc[slab:slab + 8, :] = jnp.broadcast_to(jnp.exp(g[(j + 1) * c - 1:(j + 1) * c, :]), (8, A_WIDTH))
    yield


PREP_STAGES = 6


def _rwkv_local(chunk0, par, sc):
    c = CHUNK
    prow = _iota((c, LANES), 0)
    pcol = _iota((c, LANES), 1) & (A_HEAD_DIM - 1)
    strict = pcol < prow
    incl = pcol <= prow
    head0 = _iota((c, LANES), 1) < A_HEAD_DIM
    chains = [(chunk0 + j, p) for j in range(par) for p in range(N_PAIRS)]

    def rows(ci):
        return slice(ci * c, (ci + 1) * c)

    a_dec = [sc.ad[rows(ci), _PAIRS[p]] for ci, p in chains]
    r_dec = [sc.rd[rows(ci), _PAIRS[p]] for ci, p in chains]
    b_inv = [sc.bi[rows(ci), _PAIRS[p]] for ci, p in chains]
    k_inv = [sc.ki[rows(ci), _PAIRS[p]] for ci, p in chains]
    vv = [sc.v[rows(ci), _PAIRS[p]] for ci, p in chains]

    gms = [_mm_nt(jnp.concatenate([ad, rd], axis=0),
                  jnp.concatenate([jnp.where(head0, bp, 0.0), jnp.where(head0, 0.0, bp),
                                   jnp.where(head0, kp, 0.0), jnp.where(head0, 0.0, kp)], axis=0))
           for ad, rd, bp, kp in zip(a_dec, r_dec, b_inv, k_inv)]
    yield
    n_ab = [jnp.where(strict, gm[0:c, 0:LANES], 0.0) for gm in gms]
    a_ak = [jnp.where(strict, gm[0:c, LANES:2 * LANES], 0.0) for gm in gms]
    a_rb = [jnp.where(incl, gm[c:2 * c, 0:LANES], 0.0) for gm in gms]
    a_rk = [jnp.where(incl, gm[c:2 * c, LANES:2 * LANES], 0.0) for gm in gms]
    tinv = []
    yield from _unit_lower_inverse(n_ab, tinv)
    from_v = [_mm(jnp.concatenate([ak, rk], axis=0), _bd(x)) for ak, rk, x in zip(a_ak, a_rk, vv)]
    yield
    tw = [_mm(t, jnp.concatenate([_bd(ad), _bd(fv[0:c])], axis=1))
          for t, fv, ad in zip(tinv, from_v, a_dec)]
    yield
    ab = [_mm(rb, jnp.concatenate([_bd(x[:, 0:LANES]), _bd(x[:, LANES:2 * LANES])], axis=1))
          for rb, x in zip(a_rb, tw)]
    yield
    for i, (ci, p) in enumerate(chains):
        slot = ci * N_PAIRS + p
        sc.w[rows(ci), _PAIRS[p]] = tw[i][:, 0:LANES]
        sc.q[rows(ci), _PAIRS[p]] = r_dec[i] + ab[i][:, 0:LANES]
        sc.y[rows(ci), _PAIRS[p]] = ab[i][:, LANES:2 * LANES] + from_v[i][c:2 * c]
        sc.uv[slot] = jnp.concatenate([tw[i][:, LANES:2 * LANES], vv[i]], axis=0).T
    yield


LOCAL_STAGES = 15


def _rwkv_state_step(ci, sc, bd_mask):
    c = CHUNK
    rows = slice(ci * c, (ci + 1) * c)
    dc = sc.dc[ci * 8:ci * 8 + 1, :]
    sts = [sc.s[p] for p in range(N_PAIRS)]
    zeros = jnp.zeros((c, LANES), F32)
    uvs = [_mm_nt(st, jnp.concatenate([sc.w[rows, sl], zeros], axis=0)) + sc.uv[ci * N_PAIRS + p]
           for p, (st, sl) in enumerate(zip(sts, _PAIRS))]
    ys = [_mm_nt(sc.q[rows, sl], st) for st, sl in zip(sts, _PAIRS)]
    yield
    upd = [_mm(uv, jnp.concatenate([sc.bi[rows, sl], sc.ki[rows, sl]], axis=0) * dc[:, sl])
           for uv, sl in zip(uvs, _PAIRS)]
    yield
    for p, sl in enumerate(_PAIRS):
        sc.s[p] = sts[p] * dc[:, sl] + jnp.where(bd_mask, upd[p], 0.0)
        sc.y[rows, sl] = sc.y[rows, sl] + ys[p]
    yield


STATE_STAGES = 3


def _rwkv_finish(r0, n, vec_ref, sc, ga_sc, ya_sc, head_ones2):
    rows = slice(r0, r0 + n)
    ln_w = vec_ref[5:6, :]
    ln_b = vec_ref[6:7, :]
    y = sc.y[rows, :]
    inv_n = 1.0 / A_HEAD_DIM
    mean = _headsum(y, head_ones2) * inv_n
    yield
    d = y - mean
    var = _headsum(d * d, head_ones2) * inv_n
    yield
    yn = d * lax.rsqrt(var + A_GN_EPS) * ln_w + ln_b
    ya_sc[rows, :] = _bf((yn + sc.bo[rows, :]) * _silu(ga_sc[rows, :]))
    yield


def _swa_block(j, first_tile, sink_ref, bias_ref, qb_sc, kv_sc, gb_sc, yb_sc):
    h0 = _iota((2 * BLOCK, B_KV_WIDTH), 1) < B_HEAD_DIM
    heads = [(hq // 2, hq % 2, hq // B_GROUP) for hq in range(B_HEADS)]

    def ext(straight, rolled, kvh, e):
        src = straight if kvh == e else rolled
        return _bf(jnp.where(h0, src, 0.0) if e == 0 else jnp.where(h0, 0.0, src))

    rows = slice(j * BLOCK, (j + 1) * BLOCK)
    variant = jnp.where(first_tile, 0, 1) if j == 0 else 1
    kv = kv_sc[j * BLOCK:(j + 2) * BLOCK, :]
    kc = kv[:, 0:B_KV_WIDTH]
    vc = kv[:, B_KV_WIDTH:2 * B_KV_WIDTH]
    kr = pltpu.roll(kc, B_HEAD_DIM, axis=1)
    vr = pltpu.roll(vc, B_HEAD_DIM, axis=1)
    kext = {(kvh, e): ext(kc, kr, kvh, e) for _, e, kvh in heads}
    vext = {(kvh, e): ext(vc, vr, kvh, e) for _, e, kvh in heads}
    q = [qb_sc[rows, p * LANES:(p + 1) * LANES] for p in range(B_WIDTH // LANES)]
    yield
    out = [None] * B_HEADS
    for h_lo in range(0, B_HEADS, 2):
        grp = list(enumerate(heads))[h_lo:h_lo + 2]
        s = [lax.dot_general(q[p], kext[(kvh, e)], (((1,), (1,)), ((), ())), preferred_element_type=F32)
             + bias_ref[variant, hq] for hq, (p, e, kvh) in grp]
        m = [jnp.maximum(jnp.max(x, axis=-1, keepdims=True), sink_ref[hq]) for (hq, _), x in zip(grp, s)]
        yield
        pe = [jnp.exp2(x - mx) for x, mx in zip(s, m)]
        inv = [1.0 / (jnp.sum(x, axis=-1, keepdims=True) + jnp.exp2(sink_ref[hq] - mx))
               for (hq, _), x, mx in zip(grp, pe, m)]
        yield
        for (hq, (_, e, kvh)), x, rcp in zip(grp, pe, inv):
            out[hq] = jnp.dot(_bf(x), vext[(kvh, e)], preferred_element_type=F32) * rcp
        p = h_lo // 2
        sl = slice(p * LANES, (p + 1) * LANES)
        yb_sc[rows, sl] = _bf((out[h_lo] + out[h_lo + 1]) * _silu(gb_sc[rows, sl]))
        yield


SWA_STAGES = 13


def _layer0_kernel(sink_ref, x_ref, nw_ref, w_rkv, w_lora, w_ga, w_qb, w_kvb, w_gb, mu_rkv, mu_lora, mu_ga,
                   wcomb_ref, vec_ref, bias_ref, wo_a, wo_b, pnw_ref, o_ref,
                   c_rkv, c_lora, c_ga, ga_sc, qb_sc, kv_sc, gb_sc, ya_sc, yb_sc, *rwkv_scratch, par):
    ti = pl.program_id(1)
    tc = x_ref.shape[1]
    sc = _RwkvScratch(*rwkv_scratch)
    grp = par * CHUNK
    n_grp = tc // grp
    blocks_per_grp = grp // BLOCK
    bd_mask, head_ones2 = _head_pair_consts()

    @pl.when(ti == 0)
    def _():
        c_rkv[...] = jnp.zeros_like(c_rkv)
        c_lora[...] = jnp.zeros_like(c_lora)
        c_ga[...] = jnp.zeros_like(c_ga)
        sc.s[...] = jnp.zeros_like(sc.s)
        kv_sc[0:BLOCK, :] = jnp.zeros((BLOCK, 2 * B_KV_WIDTH), F32)

    def project(g, proj):
        rows = slice(g * grp, (g + 1) * grp)
        hb = _bf(_rms(x_ref[0, rows, :], nw_ref[...]))

        def shifted(w, mu, carry, lo, hi):
            p = jnp.dot(hb, w[:, lo:hi], preferred_element_type=F32)
            prev = pltpu.roll(p, 1, axis=0)
            prev = jnp.where(_iota(p.shape, 0) == 0, carry[0:1, lo:hi], prev)
            carry[0:1, lo:hi] = p[grp - 1:grp, :]
            return p + mu[:, lo:hi] * (prev - p)

        yield
        for i, name in enumerate(("r", "k", "v")):
            proj[name] = shifted(w_rkv, mu_rkv, c_rkv, i * A_WIDTH, (i + 1) * A_WIDTH)
            yield
        proj["lo"] = shifted(w_lora, mu_lora, c_lora, 0, 2 * A_LORA)
        ga_sc[rows, :] = shifted(w_ga, mu_ga, c_ga, 0, A_WIDTH)
        yield
        qb_sc[rows, :] = _bf(jnp.dot(hb, w_qb[...], preferred_element_type=F32) * (B_HEAD_DIM ** -0.5 * LOG2E))
        kv_sc[BLOCK + g * grp:BLOCK + (g + 1) * grp, :] = jnp.dot(hb, w_kvb[...], preferred_element_type=F32)
        yield
        gb_sc[rows, :] = jnp.dot(hb, w_gb[...], preferred_element_type=F32)
        yield

    project_stages = 7

    def attend(g):
        return _chain(*[_swa_block(j, ti == 0, sink_ref, bias_ref, qb_sc, kv_sc, gb_sc, yb_sc)
                        for j in range(g * blocks_per_grp, (g + 1) * blocks_per_grp)])

    def state(g):
        return _chain(*[_rwkv_state_step(ci, sc, bd_mask) for ci in range(g * par, (g + 1) * par)])

    def finish(g):
        rows = slice(g * grp, (g + 1) * grp)
        yield from _rwkv_finish(g * grp, grp, vec_ref, sc, ga_sc, ya_sc, head_ones2)
        y = (jnp.dot(ya_sc[rows, :], wo_a[...], preferred_element_type=F32)
             + jnp.dot(yb_sc[rows, :], wo_b[...], preferred_element_type=F32))
        yield
        o_ref[0, rows, :] = x_ref[0, rows, :] + _rms(y, pnw_ref[...])
        yield

    finish_stages = 5

    def on(gen_fn, g):
        return gen_fn(g) if 0 <= g < n_grp else None

    projs = [dict() for _ in range(n_grp)]
    _run_interleaved((project(0, projs[0]), project_stages))
    for g in range(n_grp + 3):
        _run_interleaved(
            (project(g + 1, projs[g + 1]) if g + 1 < n_grp else None, project_stages),
            (_rwkv_prep(g * grp, projs[g], wcomb_ref, vec_ref, sc, head_ones2) if g < n_grp else None, PREP_STAGES),
            (_rwkv_local((g - 1) * par, par, sc) if 0 <= g - 1 < n_grp else None, LOCAL_STAGES),
            (on(attend, g - 1), SWA_STAGES * blocks_per_grp),
            (on(state, g - 2), STATE_STAGES * par),
            (on(finish, g - 3), finish_stages))
        if g - 1 == n_grp - 1:
            kv_sc[0:BLOCK, :] = kv_sc[tc:tc + BLOCK, :]


def _layer0(sinks, x, nw, ws, mus, wcomb, vecs, bias, wo_a, wo_b, pnw, tc):
    bsz, t, d = x.shape
    n_chunks = tc // CHUNK
    par = math.gcd(A_PAR, n_chunks)
    full = lambda a: pl.BlockSpec(a.shape, lambda b, i: (0,) * a.ndim)
    tok = pl.BlockSpec((1, tc, d), lambda b, i: (b, i, 0))
    widths = [w.shape[1] for w in ws]
    tile = pltpu.VMEM((tc, A_WIDTH), F32)
    tile16 = pltpu.VMEM((tc, A_WIDTH), BF16)
    scratch = [pltpu.VMEM((8, widths[0]), F32), pltpu.VMEM((8, widths[1]), F32), pltpu.VMEM((8, widths[2]), F32),
               tile,
               tile16,
               pltpu.VMEM((tc + BLOCK, 2 * B_KV_WIDTH), F32),
               tile,
               tile16, tile16,
               pltpu.VMEM((N_PAIRS, LANES, LANES), F32),
               tile, tile, tile, tile, tile,
               pltpu.VMEM((n_chunks * 8, A_WIDTH), F32),
               tile, tile, tile, tile,
               pltpu.VMEM((n_chunks * N_PAIRS, LANES, LANES), F32)]
    return pl.pallas_call(
        functools.partial(_layer0_kernel, par=par), grid=(bsz, t // tc),
        in_specs=[pl.BlockSpec(memory_space=pltpu.SMEM), tok, full(nw)] + [full(w) for w in ws]
                 + [full(m) for m in mus] + [full(wcomb), full(vecs), full(bias), full(wo_a), full(wo_b), full(pnw)],
        out_specs=tok, out_shape=jax.ShapeDtypeStruct((bsz, t, d), F32),
        scratch_shapes=scratch,
        compiler_params=_cparams(("parallel", "arbitrary")), name="layer0",
    )(sinks, x, nw, *ws, *mus, wcomb, vecs, bias, wo_a, wo_b, pnw)


def _layer1_kernel(x_ref, nw_ref, w_ref, lb_ref, cw_ref, wo_ref, pnw_ref, o_ref,
                   s_ref, qd_sc, ki_sc, v_sc, g_sc, dc_sc, o_sc, y_sc, *, par, group):
    ti = pl.program_id(1)

    @pl.when(ti == 0)
    def _():
        s_ref[...] = jnp.zeros_like(s_ref)

    tm = x_ref.shape[1]
    c = CHUNK
    grp = group * c
    n_grp = tm // grp
    l0 = lb_ref[0:1, :]
    l1 = lb_ref[1:2, :]
    m = jnp.maximum(l0, l1)
    e0 = jnp.exp(l0 - m)
    e1 = jnp.exp(l1 - m)
    s0 = e0 / (e0 + e1)
    s1 = e1 / (e0 + e1)
    lb = (s0 + s1) - s0
    causal = _iota((c, c), 1) <= _iota((c, c), 0)
    heads = [slice(hd * C_DIM, (hd + 1) * C_DIM) for hd in range(C_HEADS)]

    def project(g):
        rows = slice(g * grp, (g + 1) * grp)
        p = jnp.dot(_bf(_rms(x_ref[0, rows, :], nw_ref[...])), w_ref[...], preferred_element_type=F32)
        v_sc[rows, :] = _bf(p[:, 2 * C_WIDTH:3 * C_WIDTH])
        g_sc[rows, :] = p[:, 3 * C_WIDTH:4 * C_WIDTH]
        forget = lb + (1.0 - lb) * _sigmoid(p[:, C_WIDTH:2 * C_WIDTH])
        gc = _chunk_cumsum(jnp.log(forget), c)
        qd_sc[rows, :] = _bf(_silu(p[:, 0:C_WIDTH]) * jnp.exp(gc))
        ki_sc[rows, :] = (1.0 - forget) * jnp.exp(-gc)
        for j in range(group):
            slab = (g * group + j) * 8
            dc_sc[slab:slab + 8, :] = jnp.broadcast_to(jnp.exp(gc[(j + 1) * c - 1:(j + 1) * c, :]), (8, C_WIDTH))

    def chunks(chunk0, sts):
        chains = [(chunk0 + j, hd) for j in range(par) for hd in range(C_HEADS)]

        def rows(ci):
            return slice(ci * c, (ci + 1) * c)

        q_dec = [qd_sc[rows(ci), heads[hd]] for ci, hd in chains]
        k_inv = [ki_sc[rows(ci), heads[hd]] for ci, hd in chains]
        vv = [v_sc[rows(ci), heads[hd]] for ci, hd in chains]
        dcs = [dc_sc[ci * 8:ci * 8 + 1, heads[hd]] for ci, hd in chains]
        sc = [jnp.where(causal, _mm_nt(qd, ki), 0.0) for qd, ki in zip(q_dec, k_inv)]
        upd = [_mm(vx.astype(F32).T, ki * dc) for vx, ki, dc in zip(vv, k_inv, dcs)]
        intra = [_mm(s, vx) for s, vx in zip(sc, vv)]
        for i, (ci, hd) in enumerate(chains):
            o_sc[rows(ci), heads[hd]] = intra[i] + _mm_nt(q_dec[i], sts[hd])
            sts[hd] = sts[hd] * dcs[i] + upd[i]

    def finish(g):
        rows = slice(g * grp, (g + 1) * grp)
        for sl in heads:
            y_sc[rows, sl] = _bf(_rms(o_sc[rows, sl], cw_ref[...]) * _silu(g_sc[rows, sl]))
        y = jnp.dot(y_sc[rows, :], wo_ref[...], preferred_element_type=F32)
        o_ref[0, rows, :] = x_ref[0, rows, :] + _rms(y, pnw_ref[...])

    sts = [s_ref[hd] for hd in range(C_HEADS)]
    project(0)
    for g in range(n_grp):
        if g + 1 < n_grp:
            project(g + 1)
        for ci in range(g * group, (g + 1) * group, par):
            chunks(ci, sts)
        finish(g)
    for hd in range(C_HEADS):
        s_ref[hd] = sts[hd]


def _layer1(x, nw, w, lbs, cw, wo, pnw, tm):
    bsz, t, d = x.shape
    group = math.gcd(C_GROUP, tm // CHUNK)
    par = math.gcd(C_PAR, group)
    full = lambda a: pl.BlockSpec(a.shape, lambda b, i: (0,) * a.ndim)
    tok = pl.BlockSpec((1, tm, d), lambda b, i: (b, i, 0))
    act16 = pltpu.VMEM((tm, C_WIDTH), BF16)
    act32 = pltpu.VMEM((tm, C_WIDTH), F32)
    return pl.pallas_call(
        functools.partial(_layer1_kernel, par=par, group=group), grid=(bsz, t // tm),
        in_specs=[tok, full(nw), full(w), full(lbs), full(cw), full(wo), full(pnw)],
        out_specs=tok, out_shape=jax.ShapeDtypeStruct((bsz, t, d), F32),
        scratch_shapes=[pltpu.VMEM((C_HEADS, C_DIM, C_DIM), F32),
                        act16, act32, act16, act32,
                        pltpu.VMEM((tm // CHUNK * 8, C_WIDTH), F32),
                        act32, act16],
        compiler_params=_cparams(("parallel", "arbitrary")), name="layer1",
    )(x, nw, w, lbs, cw, wo, pnw)


def _t5_bucket(dist):
    max_exact = REL_BUCKETS // 2
    d = jnp.maximum(dist, 0)
    large = max_exact + jnp.floor(jnp.log(jnp.maximum(d, 1).astype(F32) / max_exact)
                                  / math.log(REL_MAX_DIST / max_exact)
                                  * (REL_BUCKETS - max_exact)).astype(jnp.int32)
    large = jnp.minimum(large, REL_BUCKETS - 1)
    return jnp.where(d < max_exact, d, large)


def _tile(t, want):
    return want if t % want == 0 else BLOCK


def kernel(x, rel_bias_table, lower_bounds, l0_pre_norm, l0_post_norm, l0_w_in, l0_w_out, a_mu, a_w0, a_w2, a_a0, a_a2, a_k_k, a_k_a, a_r_k, a_ln_w, a_ln_b, b_sinks, l1_pre_norm, l1_post_norm, l1_w_in, l1_w_out, c_norm_w):
    bsz, t, d = x.shape
    assert d == D_MODEL and t % BLOCK == 0
    row = lambda v: v.reshape(1, -1).astype(F32)

    c_rkv, c_lora, c_ga = 3 * A_WIDTH, 3 * A_WIDTH + 2 * A_LORA, 4 * A_WIDTH + 2 * A_LORA
    c_q, c_kv, c_gb = c_ga + B_WIDTH, c_ga + B_WIDTH + 2 * B_KV_WIDTH, c_ga + 2 * B_WIDTH + 2 * B_KV_WIDTH
    wb = _bf(l0_w_in)
    ws = [wb[:, 0:c_rkv], wb[:, c_rkv:c_lora], wb[:, c_lora:c_ga], wb[:, c_ga:c_q], wb[:, c_q:c_kv], wb[:, c_kv:c_gb]]
    mus = [row(a_mu[0:c_rkv]), row(a_mu[c_rkv:c_lora]), row(a_mu[c_lora:c_ga])]
    zeros = jnp.zeros((A_LORA, A_WIDTH), F32)
    wcomb = jnp.concatenate([jnp.concatenate([a_w2, zeros], axis=1), jnp.concatenate([zeros, a_a2], axis=1)], axis=0)
    vecs = jnp.stack([a_w0, a_a0, a_k_k, a_k_a, a_r_k.reshape(-1), a_ln_w, a_ln_b, jnp.zeros_like(a_w0)]).astype(F32)
    qi = jnp.arange(BLOCK)[:, None]
    kj = jnp.arange(2 * BLOCK)[None, :]
    dist = qi + BLOCK - kj
    onehot = (_t5_bucket(dist)[:, :, None] == jnp.arange(REL_BUCKETS)).astype(F32)
    bias = jnp.einsum("qkb,bh->hqk", onehot, rel_bias_table.astype(F32), precision=lax.Precision.HIGHEST)
    in_window = (dist >= 0) & (dist < WINDOW)
    bias = bias * LOG2E
    bias = jnp.stack([jnp.where(in_window & (kj >= BLOCK), bias, NEG_INF), jnp.where(in_window, bias, NEG_INF)])
    w_out0 = _bf(l0_w_out)
    w_in1 = _bf(l1_w_in)
    w_out1 = _bf(l1_w_out)

    x1 = _layer0(b_sinks.astype(F32) * LOG2E, x, row(l0_pre_norm), ws, mus, wcomb, vecs, bias,
                 w_out0[0:A_WIDTH], w_out0[A_WIDTH:], row(l0_post_norm), _tile(t, 512))
    return _layer1(x1, row(l1_pre_norm), w_in1, lower_bounds.astype(F32), row(c_norm_w), w_out1, row(l1_post_norm),
                   _tile(t, 512))
```

```python
import functools
import math

import jax
import jax.numpy as jnp
from jax import lax
from jax.experimental import pallas as pl
from jax.experimental.pallas import tpu as pltpu

D_MODEL = 1024
A_HEAD_DIM = 64
A_WIDTH = 512
A_LORA = 32
A_GN_EPS = 64e-5
B_HEAD_DIM = 64
B_WIDTH = 512
B_HEADS = 8
B_KV_WIDTH = 128
B_GROUP = 4
WINDOW = 128
BLOCK = 128
REL_BUCKETS = 32
REL_MAX_DIST = 128
C_HEADS = 8
C_DIM = 128
C_WIDTH = 1024
CHUNK = 64
NORM_EPS = 1e-6
NEG_INF = -1e30
LOG2E = math.log2(math.e)

LANES = 128
MXU_DEPTH = 256
N_PAIRS = A_WIDTH // LANES
A_PAR = 4
C_PAR = 2
C_GROUP = 4
VMEM_LIMIT = 56 * 1024 * 1024

F32 = jnp.float32
BF16 = jnp.bfloat16


def _bf(x):
    return x.astype(BF16)


def _mm(a, b):
    return jnp.dot(_bf(a), _bf(b), preferred_element_type=F32)


def _mm_nt(a, b):
    return lax.dot_general(_bf(a), _bf(b), (((1,), (1,)), ((), ())), preferred_element_type=F32)


def _split(x, parts):
    out = []
    rem = x
    for i in range(parts):
        hi = _bf(rem)
        out.append(hi)
        if i + 1 < parts:
            rem = rem - hi.astype(F32)
    return out


def _mm_exact_lhs(a_bf, x, parts):
    n = x.shape[1]
    xs = jnp.concatenate(_split(x, parts), axis=1)
    y = jnp.dot(a_bf, xs, preferred_element_type=F32)
    acc = y[:, 0:n]
    for i in range(1, parts):
        acc = acc + y[:, i * n:(i + 1) * n]
    return acc


def _mm_exact_rhs(x, b_stacked, parts):
    return jnp.dot(jnp.concatenate(_split(x, parts), axis=1), b_stacked, preferred_element_type=F32)


def _mm3(a, b):
    a_hi, a_lo = _split(a, 2)
    b_hi, b_lo = _split(b, 2)
    return jnp.dot(jnp.concatenate([a_hi, a_hi, a_lo], axis=1), jnp.concatenate([b_hi, b_lo, b_hi], axis=0),
                   preferred_element_type=F32)


def _iota(shape, dim):
    return lax.broadcasted_iota(jnp.int32, shape, dim)


def _sigmoid(x):
    return 0.5 * (1.0 + jnp.tanh(0.5 * x))


def _silu(x):
    return x * _sigmoid(x)


def _chunk_cumsum(x, c):
    t = x.shape[0]
    blk = min(t, MXU_DEPTH)
    i = _iota((blk, blk), 0)
    j = _iota((blk, blk), 1)
    cum = jnp.where(((i // c) == (j // c)) & (j <= i), 1.0, 0.0).astype(BF16)
    return jnp.concatenate([_mm_exact_lhs(cum, x[b * blk:(b + 1) * blk], 2) for b in range(t // blk)], axis=0)


def _rms(x, w):
    return x * lax.rsqrt(jnp.mean(x * x, axis=-1, keepdims=True) + NORM_EPS) * w


def _cparams(sem):
    return pltpu.CompilerParams(dimension_semantics=sem, vmem_limit_bytes=VMEM_LIMIT)


def _chain(*gens):
    for gen in gens:
        yield from gen


def _run_interleaved(*stages):
    live = [[gen, max(n, 1), 0] for gen, n in stages if gen is not None]
    ticks = max((n for _, n, _ in live), default=0)
    for t in range(ticks + 1):
        for item in live:
            gen, n, done = item
            want = n if t == ticks else (t + 1) * n // ticks
            while gen is not None and (done < want or t == ticks):
                try:
                    next(gen)
                    done += 1
                except StopIteration:
                    gen = None
            item[0], item[2] = gen, done


def _bd(x):
    h0 = _iota(x.shape, 1) < A_HEAD_DIM
    return jnp.concatenate([jnp.where(h0, x, 0.0), jnp.where(h0, 0.0, x)], axis=0)


def _unit_lower_inverse(ns, out):
    c = ns[0].shape[0]
    row = _iota(ns[0].shape, 0)
    col = _iota(ns[0].shape, 1) & (A_HEAD_DIM - 1)
    eye = jnp.where(col == row, 1.0, 0.0)
    first = (col == row - 1) & ((row & 1) == 1)
    ts = [eye + jnp.where(first, n, 0.0) for n in ns]
    bs = 2
    while bs < c:
        sel = ((row // (2 * bs)) == (col // (2 * bs))) & (((row // bs) & 1) == 1) & (((col // bs) & 1) == 0)
        xs = [_mm(jnp.where(sel, n, 0.0), _bd(t)) for n, t in zip(ns, ts)]
        yield
        ts = [t + _mm(t, _bd(x)) for t, x in zip(ts, xs)]
        yield
        bs *= 2
    out[:] = ts


class _RwkvScratch:
    def __init__(self, s, v, ad, rd, bi, ki, dc, q, y, w, bo, uv):
        self.s, self.v, self.ad, self.rd, self.bi, self.ki, self.dc = s, v, ad, rd, bi, ki, dc
        self.q, self.y, self.w, self.bo, self.uv = q, y, w, bo, uv


def _head_pair_consts():
    li = _iota((LANES, LANES), 0) // A_HEAD_DIM
    lj = _iota((LANES, LANES), 1) // A_HEAD_DIM
    bd_mask = li == lj
    head_ones = jnp.where(bd_mask, 1.0, 0.0).astype(BF16)
    return bd_mask, jnp.concatenate([head_ones, head_ones], axis=0)


_PAIRS = [slice(p * LANES, (p + 1) * LANES) for p in range(N_PAIRS)]


def _headsum(x, head_ones2):
    return jnp.concatenate([_mm_exact_rhs(x[:, sl], head_ones2, 2) for sl in _PAIRS], axis=1)


def _rwkv_prep(r0, proj, wcomb_ref, vec_ref, sc, head_ones2):
    r, k, v, lo = proj["r"], proj["k"], proj["v"], proj["lo"]
    n = r.shape[0]
    c = CHUNK
    rows = slice(r0, r0 + n)
    w0, a0, k_k, k_a, r_k = (vec_ref[i:i + 1, :] for i in range(5))
    z = jnp.where(_iota(lo.shape, 1) < A_LORA, jnp.tanh(lo), lo)
    wa = _mm3(z, wcomb_ref[...])
    logw = -math.exp(-0.5) * _sigmoid(w0 + wa[:, 0:A_WIDTH])
    yield
    g = _chunk_cumsum(logw, c)
    a = _sigmoid(a0 + wa[:, A_WIDTH:2 * A_WIDTH])
    yield
    kk = k * k_k
    kkn = kk * lax.rsqrt(jnp.maximum(_headsum(kk * kk, head_ones2), 1e-24))
    yield
    k2 = k * (1.0 + (a - 1.0) * k_a)
    sc.v[rows, :] = v
    sc.bo[rows, :] = _headsum(r * k2 * r_k, head_ones2) * v
    yield
    e_neg = jnp.exp(-g)
    sc.bi[rows, :] = kkn * a * e_neg
    sc.ki[rows, :] = k2 * e_neg
    yield
    sc.ad[rows, :] = -kkn * jnp.exp(g - logw)
    sc.rd[rows, :] = r * jnp.exp(g)
    for j in range(n // c):
        slab = (r0 // c + j) * 8
        sc.dc[slab:slab + 8, :] = jnp.broadcast_to(jnp.exp(g[(j + 1) * c - 1:(j + 1) * c, :]), (8, A_WIDTH))
    yield


PREP_STAGES = 6


def _rwkv_local(chunk0, par, sc):
    c = CHUNK
    prow = _iota((c, LANES), 0)
    pcol = _iota((c, LANES), 1) & (A_HEAD_DIM - 1)
    strict = pcol < prow
    incl = pcol <= prow
    head0 = _iota((c, LANES), 1) < A_HEAD_DIM
    chains = [(chunk0 + j, p) for j in range(par) for p in range(N_PAIRS)]

    def rows(ci):
        return slice(ci * c, (ci + 1) * c)

    a_dec = [sc.ad[rows(ci), _PAIRS[p]] for ci, p in chains]
    r_dec = [sc.rd[rows(ci), _PAIRS[p]] for ci, p in chains]
    b_inv = [sc.bi[rows(ci), _PAIRS[p]] for ci, p in chains]
    k_inv = [sc.ki[rows(ci), _PAIRS[p]] for ci, p in chains]
    vv = [sc.v[rows(ci), _PAIRS[p]] for ci, p in chains]

    gms = [_mm_nt(jnp.concatenate([ad, rd], axis=0),
                  jnp.concatenate([jnp.where(head0, bp, 0.0), jnp.where(head0, 0.0, bp),
                                   jnp.where(head0, kp, 0.0), jnp.where(head0, 0.0, kp)], axis=0))
           for ad, rd, bp, kp in zip(a_dec, r_dec, b_inv, k_inv)]
    yield
    n_ab = [jnp.where(strict, gm[0:c, 0:LANES], 0.0) for gm in gms]
    a_ak = [jnp.where(strict, gm[0:c, LANES:2 * LANES], 0.0) for gm in gms]
    a_rb = [jnp.where(incl, gm[c:2 * c, 0:LANES], 0.0) for gm in gms]
    a_rk = [jnp.where(incl, gm[c:2 * c, LANES:2 * LANES], 0.0) for gm in gms]
    tinv = []
    yield from _unit_lower_inverse(n_ab, tinv)
    from_v = [_mm(jnp.concatenate([ak, rk], axis=0), _bd(x)) for ak, rk, x in zip(a_ak, a_rk, vv)]
    yield
    tw = [_mm(t, jnp.concatenate([_bd(ad), _bd(fv[0:c])], axis=1))
          for t, fv, ad in zip(tinv, from_v, a_dec)]
    yield
    ab = [_mm(rb, jnp.concatenate([_bd(x[:, 0:LANES]), _bd(x[:, LANES:2 * LANES])], axis=1))
          for rb, x in zip(a_rb, tw)]
    yield
    for i, (ci, p) in enumerate(chains):
        slot = ci * N_PAIRS + p
        sc.w[rows(ci), _PAIRS[p]] = tw[i][:, 0:LANES]
        sc.q[rows(ci), _PAIRS[p]] = r_dec[i] + ab[i][:, 0:LANES]
        sc.y[rows(ci), _PAIRS[p]] = ab[i][:, LANES:2 * LANES] + from_v[i][c:2 * c]
        sc.uv[slot] = jnp.concatenate([tw[i][:, LANES:2 * LANES], vv[i]], axis=0).T
    yield


LOCAL_STAGES = 15


def _rwkv_state_step(ci, sc, bd_mask):
    c = CHUNK
    rows = slice(ci * c, (ci + 1) * c)
    dc = sc.dc[ci * 8:ci * 8 + 1, :]
    sts = [sc.s[p] for p in range(N_PAIRS)]
    zeros = jnp.zeros((c, LANES), F32)
    uvs = [_mm_nt(st, jnp.concatenate([sc.w[rows, sl], zeros], axis=0)) + sc.uv[ci * N_PAIRS + p]
           for p, (st, sl) in enumerate(zip(sts, _PAIRS))]
    ys = [_mm_nt(sc.q[rows, sl], st) for st, sl in zip(sts, _PAIRS)]
    yield
    upd = [_mm(uv, jnp.concatenate([sc.bi[rows, sl], sc.ki[rows, sl]], axis=0) * dc[:, sl])
           for uv, sl in zip(uvs, _PAIRS)]
    yield
    for p, sl in enumerate(_PAIRS):
        sc.s[p] = sts[p] * dc[:, sl] + jnp.where(bd_mask, upd[p], 0.0)
        sc.y[rows, sl] = sc.y[rows, sl] + ys[p]
    yield


STATE_STAGES = 3


def _rwkv_finish(r0, n, vec_ref, sc, ga_sc, ya_sc, head_ones2):
    rows = slice(r0, r0 + n)
    ln_w = vec_ref[5:6, :]
    ln_b = vec_ref[6:7, :]
    y = sc.y[rows, :]
    inv_n = 1.0 / A_HEAD_DIM
    mean = _headsum(y, head_ones2) * inv_n
    yield
    d = y - mean
    var = _headsum(d * d, head_ones2) * inv_n
    yield
    yn = d * lax.rsqrt(var + A_GN_EPS) * ln_w + ln_b
    ya_sc[rows, :] = _bf((yn + sc.bo[rows, :]) * _silu(ga_sc[rows, :]))
    yield


def _swa_block(j, first_tile, sink_ref, bias_ref, qb_sc, kv_sc, gb_sc, yb_sc):
    h0 = _iota((2 * BLOCK, B_KV_WIDTH), 1) < B_HEAD_DIM
    heads = [(hq // 2, hq % 2, hq // B_GROUP) for hq in range(B_HEADS)]

    def ext(straight, rolled, kvh, e):
        src = straight if kvh == e else rolled
        return _bf(jnp.where(h0, src, 0.0) if e == 0 else jnp.where(h0, 0.0, src))

    rows = slice(j * BLOCK, (j + 1) * BLOCK)
    variant = jnp.where(first_tile, 0, 1) if j == 0 else 1
    kv = kv_sc[j * BLOCK:(j + 2) * BLOCK, :]
    kc = kv[:, 0:B_KV_WIDTH]
    vc = kv[:, B_KV_WIDTH:2 * B_KV_WIDTH]
    kr = pltpu.roll(kc, B_HEAD_DIM, axis=1)
    vr = pltpu.roll(vc, B_HEAD_DIM, axis=1)
    kext = {(kvh, e): ext(kc, kr, kvh, e) for _, e, kvh in heads}
    vext = {(kvh, e): ext(vc, vr, kvh, e) for _, e, kvh in heads}
    q = [qb_sc[rows, p * LANES:(p + 1) * LANES] for p in range(B_WIDTH // LANES)]
    yield
    out = [None] * B_HEADS
    for h_lo in range(0, B_HEADS, 2):
        grp = list(enumerate(heads))[h_lo:h_lo + 2]
        s = [lax.dot_general(q[p], kext[(kvh, e)], (((1,), (1,)), ((), ())), preferred_element_type=F32)
             + bias_ref[variant, hq] for hq, (p, e, kvh) in grp]
        m = [jnp.maximum(jnp.max(x, axis=-1, keepdims=True), sink_ref[hq]) for (hq, _), x in zip(grp, s)]
        yield
        pe = [jnp.exp2(x - mx) for x, mx in zip(s, m)]
        inv = [1.0 / (jnp.sum(x, axis=-1, keepdims=True) + jnp.exp2(sink_ref[hq] - mx))
               for (hq, _), x, mx in zip(grp, pe, m)]
        yield
        for (hq, (_, e, kvh)), x, rcp in zip(grp, pe, inv):
            out[hq] = jnp.dot(_bf(x), vext[(kvh, e)], preferred_element_type=F32) * rcp
        p = h_lo // 2
        sl = slice(p * LANES, (p + 1) * LANES)
        yb_sc[rows, sl] = _bf((out[h_lo] + out[h_lo + 1]) * _silu(gb_sc[rows, sl]))
        yield


SWA_STAGES = 13


def _layer0_kernel(sink_ref, x_ref, nw_ref, w_rkv, w_lora, w_ga, w_qb, w_kvb, w_gb, mu_rkv, mu_lora, mu_ga,
                   wcomb_ref, vec_ref, bias_ref, wo_a, wo_b, pnw_ref, o_ref,
                   c_rkv, c_lora, c_ga, ga_sc, qb_sc, kv_sc, gb_sc, ya_sc, yb_sc, *rwkv_scratch, par):
    ti = pl.program_id(1)
    tc = x_ref.shape[1]
    sc = _RwkvScratch(*rwkv_scratch)
    grp = par * CHUNK
    n_grp = tc // grp
    blocks_per_grp = grp // BLOCK
    bd_mask, head_ones2 = _head_pair_consts()

    @pl.when(ti == 0)
    def _():
        c_rkv[...] = jnp.zeros_like(c_rkv)
        c_lora[...] = jnp.zeros_like(c_lora)
        c_ga[...] = jnp.zeros_like(c_ga)
        sc.s[...] = jnp.zeros_like(sc.s)
        kv_sc[0:BLOCK, :] = jnp.zeros((BLOCK, 2 * B_KV_WIDTH), F32)

    def project(g, proj):
        rows = slice(g * grp, (g + 1) * grp)
        hb = _bf(_rms(x_ref[0, rows, :], nw_ref[...]))

        def shifted(w, mu, carry, lo, hi):
            p = jnp.dot(hb, w[:, lo:hi], preferred_element_type=F32)
            prev = pltpu.roll(p, 1, axis=0)
            prev = jnp.where(_iota(p.shape, 0) == 0, carry[0:1, lo:hi], prev)
            carry[0:1, lo:hi] = p[grp - 1:grp, :]
            return p + mu[:, lo:hi] * (prev - p)

        yield
        for i, name in enumerate(("r", "k", "v")):
            proj[name] = shifted(w_rkv, mu_rkv, c_rkv, i * A_WIDTH, (i + 1) * A_WIDTH)
            yield
        proj["lo"] = shifted(w_lora, mu_lora, c_lora, 0, 2 * A_LORA)
        ga_sc[rows, :] = shifted(w_ga, mu_ga, c_ga, 0, A_WIDTH)
        yield
        qb_sc[rows, :] = _bf(jnp.dot(hb, w_qb[...], preferred_element_type=F32) * (B_HEAD_DIM ** -0.5 * LOG2E))
        kv_sc[BLOCK + g * grp:BLOCK + (g + 1) * grp, :] = jnp.dot(hb, w_kvb[...], preferred_element_type=F32)
        yield
        gb_sc[rows, :] = jnp.dot(hb, w_gb[...], preferred_element_type=F32)
        yield

    project_stages = 7

    def attend(g):
        return _chain(*[_swa_block(j, ti == 0, sink_ref, bias_ref, qb_sc, kv_sc, gb_sc, yb_sc)
                        for j in range(g * blocks_per_grp, (g + 1) * blocks_per_grp)])

    def state(g):
        return _chain(*[_rwkv_state_step(ci, sc, bd_mask) for ci in range(g * par, (g + 1) * par)])

    def finish(g):
        rows = slice(g * grp, (g + 1) * grp)
        yield from _rwkv_finish(g * grp, grp, vec_ref, sc, ga_sc, ya_sc, head_ones2)
        y = (jnp.dot(ya_sc[rows, :], wo_a[...], preferred_element_type=F32)
             + jnp.dot(yb_sc[rows, :], wo_b[...], preferred_element_type=F32))
        yield
        o_ref[0, rows, :] = x_ref[0, rows, :] + _rms(y, pnw_ref[...])
        yield

    finish_stages = 5

    def on(gen_fn, g):
        return gen_fn(g) if 0 <= g < n_grp else None

    projs = [dict() for _ in range(n_grp)]
    _run_interleaved((project(0, projs[0]), project_stages))
    for g in range(n_grp + 3):
        _run_interleaved(
            (project(g + 1, projs[g + 1]) if g + 1 < n_grp else None, project_stages),
            (_rwkv_prep(g * grp, projs[g], wcomb_ref, vec_ref, sc, head_ones2) if g < n_grp else None, PREP_STAGES),
            (_rwkv_local((g - 1) * par, par, sc) if 0 <= g - 1 < n_grp else None, LOCAL_STAGES),
            (on(attend, g - 1), SWA_STAGES * blocks_per_grp),
            (on(state, g - 2), STATE_STAGES * par),
            (on(finish, g - 3), finish_stages))
        if g - 1 == n_grp - 1:
            kv_sc[0:BLOCK, :] = kv_sc[tc:tc + BLOCK, :]


def _layer0(sinks, x, nw, ws, mus, wcomb, vecs, bias, wo_a, wo_b, pnw, tc):
    bsz, t, d = x.shape
    n_chunks = tc // CHUNK
    par = math.gcd(A_PAR, n_chunks)
    full = lambda a: pl.BlockSpec(a.shape, lambda b, i: (0,) * a.ndim)
    tok = pl.BlockSpec((1, tc, d), lambda b, i: (b, i, 0))
    widths = [w.shape[1] for w in ws]
    tile = pltpu.VMEM((tc, A_WIDTH), F32)
    tile16 = pltpu.VMEM((tc, A_WIDTH), BF16)
    scratch = [pltpu.VMEM((8, widths[0]), F32), pltpu.VMEM((8, widths[1]), F32), pltpu.VMEM((8, widths[2]), F32),
               tile,
               tile16,
               pltpu.VMEM((tc + BLOCK, 2 * B_KV_WIDTH), F32),
               tile,
               tile16, tile16,
               pltpu.VMEM((N_PAIRS, LANES, LANES), F32),
               tile, tile, tile, tile, tile,
               pltpu.VMEM((n_chunks * 8, A_WIDTH), F32),
               tile, tile, tile, tile,
               pltpu.VMEM((n_chunks * N_PAIRS, LANES, LANES), F32)]
    return pl.pallas_call(
        functools.partial(_layer0_kernel, par=par), grid=(bsz, t // tc),
        in_specs=[pl.BlockSpec(memory_space=pltpu.SMEM), tok, full(nw)] + [full(w) for w in ws]
                 + [full(m) for m in mus] + [full(wcomb), full(vecs), full(bias), full(wo_a), full(wo_b), full(pnw)],
        out_specs=tok, out_shape=jax.ShapeDtypeStruct((bsz, t, d), F32),
        scratch_shapes=scratch,
        compiler_params=_cparams(("parallel", "arbitrary")), name="layer0",
    )(sinks, x, nw, *ws, *mus, wcomb, vecs, bias, wo_a, wo_b, pnw)


def _layer1_kernel(x_ref, nw_ref, w_ref, lb_ref, cw_ref, wo_ref, pnw_ref, o_ref,
                   s_ref, qd_sc, ki_sc, v_sc, g_sc, dc_sc, o_sc, y_sc, *, par, group):
    ti = pl.program_id(1)

    @pl.when(ti == 0)
    def _():
        s_ref[...] = jnp.zeros_like(s_ref)

    tm = x_ref.shape[1]
    c = CHUNK
    grp = group * c
    n_grp = tm // grp
    l0 = lb_ref[0:1, :]
    l1 = lb_ref[1:2, :]
    m = jnp.maximum(l0, l1)
    e0 = jnp.exp(l0 - m)
    e1 = jnp.exp(l1 - m)
    s0 = e0 / (e0 + e1)
    s1 = e1 / (e0 + e1)
    lb = (s0 + s1) - s0
    causal = _iota((c, c), 1) <= _iota((c, c), 0)
    heads = [slice(hd * C_DIM, (hd + 1) * C_DIM) for hd in range(C_HEADS)]

    def project(g):
        rows = slice(g * grp, (g + 1) * grp)
        p = jnp.dot(_bf(_rms(x_ref[0, rows, :], nw_ref[...])), w_ref[...], preferred_element_type=F32)
        v_sc[rows, :] = _bf(p[:, 2 * C_WIDTH:3 * C_WIDTH])
        g_sc[rows, :] = p[:, 3 * C_WIDTH:4 * C_WIDTH]
        forget = lb + (1.0 - lb) * _sigmoid(p[:, C_WIDTH:2 * C_WIDTH])
        gc = _chunk_cumsum(jnp.log(forget), c)
        qd_sc[rows, :] = _bf(_silu(p[:, 0:C_WIDTH]) * jnp.exp(gc))
        ki_sc[rows, :] = (1.0 - forget) * jnp.exp(-gc)
        for j in range(group):
            slab = (g * group + j) * 8
            dc_sc[slab:slab + 8, :] = jnp.broadcast_to(jnp.exp(gc[(j + 1) * c - 1:(j + 1) * c, :]), (8, C_WIDTH))

    def chunks(chunk0, sts):
        chains = [(chunk0 + j, hd) for j in range(par) for hd in range(C_HEADS)]

        def rows(ci):
            return slice(ci * c, (ci + 1) * c)

        q_dec = [qd_sc[rows(ci), heads[hd]] for ci, hd in chains]
        k_inv = [ki_sc[rows(ci), heads[hd]] for ci, hd in chains]
        vv = [v_sc[rows(ci), heads[hd]] for ci, hd in chains]
        dcs = [dc_sc[ci * 8:ci * 8 + 1, heads[hd]] for ci, hd in chains]
        sc = [jnp.where(causal, _mm_nt(qd, ki), 0.0) for qd, ki in zip(q_dec, k_inv)]
        upd = [_mm(vx.astype(F32).T, ki * dc) for vx, ki, dc in zip(vv, k_inv, dcs)]
        intra = [_mm(s, vx) for s, vx in zip(sc, vv)]
        for i, (ci, hd) in enumerate(chains):
            o_sc[rows(ci), heads[hd]] = intra[i] + _mm_nt(q_dec[i], sts[hd])
            sts[hd] = sts[hd] * dcs[i] + upd[i]

    def finish(g):
        rows = slice(g * grp, (g + 1) * grp)
        for sl in heads:
            y_sc[rows, sl] = _bf(_rms(o_sc[rows, sl], cw_ref[...]) * _silu(g_sc[rows, sl]))
        y = jnp.dot(y_sc[rows, :], wo_ref[...], preferred_element_type=F32)
        o_ref[0, rows, :] = x_ref[0, rows, :] + _rms(y, pnw_ref[...])

    sts = [s_ref[hd] for hd in range(C_HEADS)]
    project(0)
    for g in range(n_grp):
        if g + 1 < n_grp:
            project(g + 1)
        for ci in range(g * group, (g + 1) * group, par):
            chunks(ci, sts)
        finish(g)
    for hd in range(C_HEADS):
        s_ref[hd] = sts[hd]


def _layer1(x, nw, w, lbs, cw, wo, pnw, tm):
    bsz, t, d = x.shape
    group = math.gcd(C_GROUP, tm // CHUNK)
    par = math.gcd(C_PAR, group)
    full = lambda a: pl.BlockSpec(a.shape, lambda b, i: (0,) * a.ndim)
    once = lambda a: pl.BlockSpec(a.shape, lambda b, i: (0,) * a.ndim, pipeline_mode=pl.Buffered(1))
    tok = pl.BlockSpec((1, tm, d), lambda b, i: (b, i, 0))
    act16 = pltpu.VMEM((tm, C_WIDTH), BF16)
    act32 = pltpu.VMEM((tm, C_WIDTH), F32)
    return pl.pallas_call(
        functools.partial(_layer1_kernel, par=par, group=group), grid=(bsz, t // tm),
        in_specs=[tok, full(nw), once(w), full(lbs), full(cw), once(wo), full(pnw)],
        out_specs=tok, out_shape=jax.ShapeDtypeStruct((bsz, t, d), F32),
        scratch_shapes=[pltpu.VMEM((C_HEADS, C_DIM, C_DIM), F32),
                        act16, act32, act16, act32,
                        pltpu.VMEM((tm // CHUNK * 8, C_WIDTH), F32),
                        act32, act16],
        compiler_params=_cparams(("parallel", "arbitrary")), name="layer1",
    )(x, nw, w, lbs, cw, wo, pnw)


def _t5_bucket(dist):
    max_exact = REL_BUCKETS // 2
    d = jnp.maximum(dist, 0)
    large = max_exact + jnp.floor(jnp.log(jnp.maximum(d, 1).astype(F32) / max_exact)
                                  / math.log(REL_MAX_DIST / max_exact)
                                  * (REL_BUCKETS - max_exact)).astype(jnp.int32)
    large = jnp.minimum(large, REL_BUCKETS - 1)
    return jnp.where(d < max_exact, d, large)


def _tile(t, want):
    return want if t % want == 0 else BLOCK


def kernel(x, rel_bias_table, lower_bounds, l0_pre_norm, l0_post_norm, l0_w_in, l0_w_out, a_mu, a_w0, a_w2, a_a0, a_a2, a_k_k, a_k_a, a_r_k, a_ln_w, a_ln_b, b_sinks, l1_pre_norm, l1_post_norm, l1_w_in, l1_w_out, c_norm_w):
    bsz, t, d = x.shape
    assert d == D_MODEL and t % BLOCK == 0
    row = lambda v: v.reshape(1, -1).astype(F32)

    c_rkv, c_lora, c_ga = 3 * A_WIDTH, 3 * A_WIDTH + 2 * A_LORA, 4 * A_WIDTH + 2 * A_LORA
    c_q, c_kv, c_gb = c_ga + B_WIDTH, c_ga + B_WIDTH + 2 * B_KV_WIDTH, c_ga + 2 * B_WIDTH + 2 * B_KV_WIDTH
    wb = _bf(l0_w_in)
    ws = [wb[:, 0:c_rkv], wb[:, c_rkv:c_lora], wb[:, c_lora:c_ga], wb[:, c_ga:c_q], wb[:, c_q:c_kv], wb[:, c_kv:c_gb]]
    mus = [row(a_mu[0:c_rkv]), row(a_mu[c_rkv:c_lora]), row(a_mu[c_lora:c_ga])]
    zeros = jnp.zeros((A_LORA, A_WIDTH), F32)
    wcomb = jnp.concatenate([jnp.concatenate([a_w2, zeros], axis=1), jnp.concatenate([zeros, a_a2], axis=1)], axis=0)
    vecs = jnp.stack([a_w0, a_a0, a_k_k, a_k_a, a_r_k.reshape(-1), a_ln_w, a_ln_b, jnp.zeros_like(a_w0)]).astype(F32)
    qi = jnp.arange(BLOCK)[:, None]
    kj = jnp.arange(2 * BLOCK)[None, :]
    dist = qi + BLOCK - kj
    onehot = (_t5_bucket(dist)[:, :, None] == jnp.arange(REL_BUCKETS)).astype(F32)
    bias = jnp.einsum("qkb,bh->hqk", onehot, rel_bias_table.astype(F32), precision=lax.Precision.HIGHEST)
    in_window = (dist >= 0) & (dist < WINDOW)
    bias = bias * LOG2E
    bias = jnp.stack([jnp.where(in_window & (kj >= BLOCK), bias, NEG_INF), jnp.where(in_window, bias, NEG_INF)])
    w_out0 = _bf(l0_w_out)
    w_in1 = _bf(l1_w_in)
    w_out1 = _bf(l1_w_out)

    x1 = _layer0(b_sinks.astype(F32) * LOG2E, x, row(l0_pre_norm), ws, mus, wcomb, vecs, bias,
                 w_out0[0:A_WIDTH], w_out0[A_WIDTH:], row(l0_post_norm), _tile(t, 512))
    return _layer1(x1, row(l1_pre_norm), w_in1, lower_bounds.astype(F32), row(c_norm_w), w_out1, row(l1_post_norm),
                   _tile(t, 1024))
```

```python
import functools
import math

import jax
import jax.numpy as jnp
from jax import lax
from jax.experimental import pallas as pl
from jax.experimental.pallas import tpu as pltpu

D_MODEL = 1024
A_HEAD_DIM = 64
A_WIDTH = 512
A_LORA = 32
A_GN_EPS = 64e-5
B_HEAD_DIM = 64
B_WIDTH = 512
B_HEADS = 8
B_KV_WIDTH = 128
B_GROUP = 4
WINDOW = 128
BLOCK = 128
REL_BUCKETS = 32
REL_MAX_DIST = 128
C_HEADS = 8
C_DIM = 128
C_WIDTH = 1024
CHUNK = 64
NORM_EPS = 1e-6
NEG_INF = -1e30
LOG2E = math.log2(math.e)

LANES = 128
MXU_DEPTH = 256
N_PAIRS = A_WIDTH // LANES
A_PAR = 4
C_PAR = 2
C_GROUP = 4
VMEM_LIMIT = 56 * 1024 * 1024

F32 = jnp.float32
BF16 = jnp.bfloat16


def _bf(x):
    return x.astype(BF16)


def _mm(a, b):
    return jnp.dot(_bf(a), _bf(b), preferred_element_type=F32)


def _mm_nt(a, b):
    return lax.dot_general(_bf(a), _bf(b), (((1,), (1,)), ((), ())), preferred_element_type=F32)


def _split(x, parts):
    out = []
    rem = x
    for i in range(parts):
        hi = _bf(rem)
        out.append(hi)
        if i + 1 < parts:
            rem = rem - hi.astype(F32)
    return out


def _mm_exact_lhs(a_bf, x, parts):
    n = x.shape[1]
    xs = jnp.concatenate(_split(x, parts), axis=1)
    y = jnp.dot(a_bf, xs, preferred_element_type=F32)
    acc = y[:, 0:n]
    for i in range(1, parts):
        acc = acc + y[:, i * n:(i + 1) * n]
    return acc


def _mm_exact_rhs(x, b_stacked, parts):
    return jnp.dot(jnp.concatenate(_split(x, parts), axis=1), b_stacked, preferred_element_type=F32)


def _mm3(a, b):
    a_hi, a_lo = _split(a, 2)
    b_hi, b_lo = _split(b, 2)
    return jnp.dot(jnp.concatenate([a_hi, a_hi, a_lo], axis=1), jnp.concatenate([b_hi, b_lo, b_hi], axis=0),
                   preferred_element_type=F32)


def _iota(shape, dim):
    return lax.broadcasted_iota(jnp.int32, shape, dim)


def _sigmoid(x):
    return 0.5 * (1.0 + jnp.tanh(0.5 * x))


def _silu(x):
    h = 0.5 * x
    return h + h * jnp.tanh(h)


def _chunk_cumsum(x, c):
    t = x.shape[0]
    blk = min(t, MXU_DEPTH)
    i = _iota((blk, blk), 0)
    j = _iota((blk, blk), 1)
    cum = jnp.where(((i // c) == (j // c)) & (j <= i), 1.0, 0.0).astype(BF16)
    return jnp.concatenate([_mm_exact_lhs(cum, x[b * blk:(b + 1) * blk], 2) for b in range(t // blk)], axis=0)


def _rms(x, w):
    return x * lax.rsqrt(jnp.mean(x * x, axis=-1, keepdims=True) + NORM_EPS) * w


def _cparams(sem):
    return pltpu.CompilerParams(dimension_semantics=sem, vmem_limit_bytes=VMEM_LIMIT)


def _chain(*gens):
    for gen in gens:
        yield from gen


def _run_interleaved(*stages):
    live = [[gen, max(n, 1), 0] for gen, n in stages if gen is not None]
    ticks = max((n for _, n, _ in live), default=0)
    for t in range(ticks + 1):
        for item in live:
            gen, n, done = item
            want = n if t == ticks else (t + 1) * n // ticks
            while gen is not None and (done < want or t == ticks):
                try:
                    next(gen)
                    done += 1
                except StopIteration:
                    gen = None
            item[0], item[2] = gen, done


def _bd(x):
    h0 = _iota(x.shape, 1) < A_HEAD_DIM
    return jnp.concatenate([jnp.where(h0, x, 0.0), jnp.where(h0, 0.0, x)], axis=0)


def _unit_lower_inverse(ns, out):
    c = ns[0].shape[0]
    row = _iota(ns[0].shape, 0)
    col = _iota(ns[0].shape, 1) & (A_HEAD_DIM - 1)
    eye = jnp.where(col == row, 1.0, 0.0)
    first = (col == row - 1) & ((row & 1) == 1)
    ts = [eye + jnp.where(first, n, 0.0) for n in ns]
    bs = 2
    while bs < c:
        sel = ((row // (2 * bs)) == (col // (2 * bs))) & (((row // bs) & 1) == 1) & (((col // bs) & 1) == 0)
        xs = [_mm(jnp.where(sel, n, 0.0), _bd(t)) for n, t in zip(ns, ts)]
        yield
        ts = [t + _mm(t, _bd(x)) for t, x in zip(ts, xs)]
        yield
        bs *= 2
    out[:] = ts


class _RwkvScratch:
    def __init__(self, s, v, ad, rd, bi, ki, dc, q, y, w, bo, uv):
        self.s, self.v, self.ad, self.rd, self.bi, self.ki, self.dc = s, v, ad, rd, bi, ki, dc
        self.q, self.y, self.w, self.bo, self.uv = q, y, w, bo, uv


def _head_pair_consts():
    li = _iota((LANES, LANES), 0) // A_HEAD_DIM
    lj = _iota((LANES, LANES), 1) // A_HEAD_DIM
    bd_mask = li == lj
    head_ones = jnp.where(bd_mask, 1.0, 0.0).astype(BF16)
    return bd_mask, jnp.concatenate([head_ones, head_ones], axis=0)


_PAIRS = [slice(p * LANES, (p + 1) * LANES) for p in range(N_PAIRS)]


def _headsum(x, head_ones2):
    return jnp.concatenate([_mm_exact_rhs(x[:, sl], head_ones2, 2) for sl in _PAIRS], axis=1)


def _rwkv_prep(r0, proj, wcomb_ref, vec_ref, sc, head_ones2):
    r, k, v, lo = proj["r"], proj["k"], proj["v"], proj["lo"]
    n = r.shape[0]
    c = CHUNK
    rows = slice(r0, r0 + n)
    w0, a0, k_k, k_a, r_k = (vec_ref[i:i + 1, :] for i in range(5))
    z = jnp.where(_iota(lo.shape, 1) < A_LORA, jnp.tanh(lo), lo)
    wa = _mm3(z, wcomb_ref[...])
    half = -0.5 * math.exp(-0.5) * LOG2E
    logw = half + half * jnp.tanh(0.5 * (w0 + wa[:, 0:A_WIDTH]))
    yield
    g = _chunk_cumsum(logw, c)
    a = _sigmoid(a0 + wa[:, A_WIDTH:2 * A_WIDTH])
    yield
    kk = k * k_k
    kkn = kk * lax.rsqrt(jnp.maximum(_headsum(kk * kk, head_ones2), 1e-24))
    yield
    k2 = k * (1.0 + (a - 1.0) * k_a)
    sc.v[rows, :] = v
    sc.bo[rows, :] = _headsum(r * k2 * r_k, head_ones2) * v
    yield
    e_neg = jnp.exp2(-g)
    sc.bi[rows, :] = kkn * a * e_neg
    sc.ki[rows, :] = k2 * e_neg
    yield
    sc.ad[rows, :] = -kkn * jnp.exp2(g - logw)
    sc.rd[rows, :] = r * jnp.exp2(g)
    for j in range(n // c):
        slab = (r0 // c + j) * 8
        sc.dc[slab:slab + 8, :] = jnp.broadcast_to(jnp.exp2(g[(j + 1) * c - 1:(j + 1) * c, :]), (8, A_WIDTH))
    yield


PREP_STAGES = 6


def _rwkv_local(chunk0, par, sc):
    c = CHUNK
    prow = _iota((c, LANES), 0)
    pcol = _iota((c, LANES), 1) & (A_HEAD_DIM - 1)
    strict = pcol < prow
    incl = pcol <= prow
    head0 = _iota((c, LANES), 1) < A_HEAD_DIM
    chains = [(chunk0 + j, p) for j in range(par) for p in range(N_PAIRS)]

    def rows(ci):
        return slice(ci * c, (ci + 1) * c)

    a_dec = [sc.ad[rows(ci), _PAIRS[p]] for ci, p in chains]
    r_dec = [sc.rd[rows(ci), _PAIRS[p]] for ci, p in chains]
    b_inv = [sc.bi[rows(ci), _PAIRS[p]] for ci, p in chains]
    k_inv = [sc.ki[rows(ci), _PAIRS[p]] for ci, p in chains]
    vv = [sc.v[rows(ci), _PAIRS[p]] for ci, p in chains]

    gms = [_mm_nt(jnp.concatenate([ad, rd], axis=0),
                  jnp.concatenate([jnp.where(head0, bp, 0.0), jnp.where(head0, 0.0, bp),
                                   jnp.where(head0, kp, 0.0), jnp.where(head0, 0.0, kp)], axis=0))
           for ad, rd, bp, kp in zip(a_dec, r_dec, b_inv, k_inv)]
    yield
    n_ab = [jnp.where(strict, gm[0:c, 0:LANES], 0.0) for gm in gms]
    a_ak = [jnp.where(strict, gm[0:c, LANES:2 * LANES], 0.0) for gm in gms]
    a_rb = [jnp.where(incl, gm[c:2 * c, 0:LANES], 0.0) for gm in gms]
    a_rk = [jnp.where(incl, gm[c:2 * c, LANES:2 * LANES], 0.0) for gm in gms]
    tinv = []
    yield from _unit_lower_inverse(n_ab, tinv)
    from_v = [_mm(jnp.concatenate([ak, rk], axis=0), _bd(x)) for ak, rk, x in zip(a_ak, a_rk, vv)]
    yield
    tw = [_mm(t, jnp.concatenate([_bd(ad), _bd(fv[0:c])], axis=1))
          for t, fv, ad in zip(tinv, from_v, a_dec)]
    yield
    ab = [_mm(rb, jnp.concatenate([_bd(x[:, 0:LANES]), _bd(x[:, LANES:2 * LANES])], axis=1))
          for rb, x in zip(a_rb, tw)]
    yield
    for i, (ci, p) in enumerate(chains):
        slot = ci * N_PAIRS + p
        sc.w[rows(ci), _PAIRS[p]] = tw[i][:, 0:LANES]
        sc.q[rows(ci), _PAIRS[p]] = r_dec[i] + ab[i][:, 0:LANES]
        sc.y[rows(ci), _PAIRS[p]] = ab[i][:, LANES:2 * LANES] + from_v[i][c:2 * c]
        sc.uv[slot] = jnp.concatenate([tw[i][:, LANES:2 * LANES], vv[i]], axis=0).T
    yield


LOCAL_STAGES = 15


def _rwkv_state_step(ci, sc, bd_mask):
    c = CHUNK
    rows = slice(ci * c, (ci + 1) * c)
    dc = sc.dc[ci * 8:ci * 8 + 1, :]
    sts = [sc.s[p] for p in range(N_PAIRS)]
    zeros = jnp.zeros((c, LANES), F32)
    uvs = [_mm_nt(st, jnp.concatenate([sc.w[rows, sl], zeros], axis=0)) + sc.uv[ci * N_PAIRS + p]
           for p, (st, sl) in enumerate(zip(sts, _PAIRS))]
    ys = [_mm_nt(sc.q[rows, sl], st) for st, sl in zip(sts, _PAIRS)]
    yield
    upd = [_mm(uv, jnp.concatenate([sc.bi[rows, sl], sc.ki[rows, sl]], axis=0) * dc[:, sl])
           for uv, sl in zip(uvs, _PAIRS)]
    yield
    for p, sl in enumerate(_PAIRS):
        sc.s[p] = sts[p] * dc[:, sl] + jnp.where(bd_mask, upd[p], 0.0)
        sc.y[rows, sl] = sc.y[rows, sl] + ys[p]
    yield


STATE_STAGES = 3


def _rwkv_finish(r0, n, vec_ref, sc, ga_sc, ya_sc, head_ones2):
    rows = slice(r0, r0 + n)
    ln_w = vec_ref[5:6, :]
    ln_b = vec_ref[6:7, :]
    y = sc.y[rows, :]
    inv_n = 1.0 / A_HEAD_DIM
    mean = _headsum(y, head_ones2) * inv_n
    yield
    d = y - mean
    var = _headsum(d * d, head_ones2) * inv_n
    yield
    yn = d * lax.rsqrt(var + A_GN_EPS) * ln_w + ln_b
    ya_sc[rows, :] = _bf((yn + sc.bo[rows, :]) * _silu(ga_sc[rows, :]))
    yield


def _swa_block(j, first_tile, sink_ref, bias_ref, qb_sc, kv_sc, gb_sc, yb_sc):
    h0 = _iota((2 * BLOCK, B_KV_WIDTH), 1) < B_HEAD_DIM
    heads = [(hq // 2, hq % 2, hq // B_GROUP) for hq in range(B_HEADS)]

    def ext(straight, rolled, kvh, e):
        src = straight if kvh == e else rolled
        return _bf(jnp.where(h0, src, 0.0) if e == 0 else jnp.where(h0, 0.0, src))

    rows = slice(j * BLOCK, (j + 1) * BLOCK)
    variant = jnp.where(first_tile, 0, 1) if j == 0 else 1
    kv = kv_sc[j * BLOCK:(j + 2) * BLOCK, :]
    kc = kv[:, 0:B_KV_WIDTH]
    vc = kv[:, B_KV_WIDTH:2 * B_KV_WIDTH]
    kr = pltpu.roll(kc, B_HEAD_DIM, axis=1)
    vr = pltpu.roll(vc, B_HEAD_DIM, axis=1)
    kext = {(kvh, e): ext(kc, kr, kvh, e) for _, e, kvh in heads}
    vext = {(kvh, e): ext(vc, vr, kvh, e) for _, e, kvh in heads}
    q = [qb_sc[rows, p * LANES:(p + 1) * LANES] for p in range(B_WIDTH // LANES)]
    yield
    out = [None] * B_HEADS
    for h_lo in range(0, B_HEADS, 2):
        grp = list(enumerate(heads))[h_lo:h_lo + 2]
        s = [lax.dot_general(q[p], kext[(kvh, e)], (((1,), (1,)), ((), ())), preferred_element_type=F32)
             + bias_ref[variant, hq] for hq, (p, e, kvh) in grp]
        m = [jnp.maximum(jnp.max(x, axis=-1, keepdims=True), sink_ref[hq]) for (hq, _), x in zip(grp, s)]
        yield
        pe = [jnp.exp2(x - mx) for x, mx in zip(s, m)]
        inv = [1.0 / (jnp.sum(x, axis=-1, keepdims=True) + jnp.exp2(sink_ref[hq] - mx))
               for (hq, _), x, mx in zip(grp, pe, m)]
        yield
        for (hq, (_, e, kvh)), x, rcp in zip(grp, pe, inv):
            out[hq] = jnp.dot(_bf(x), vext[(kvh, e)], preferred_element_type=F32) * rcp
        p = h_lo // 2
        sl = slice(p * LANES, (p + 1) * LANES)
        yb_sc[rows, sl] = _bf((out[h_lo] + out[h_lo + 1]) * _silu(gb_sc[rows, sl]))
        yield


SWA_STAGES = 13


def _layer0_kernel(sink_ref, x_ref, nw_ref, w_rkv, w_lora, w_ga, w_qb, w_kvb, w_gb, mu_rkv, mu_lora, mu_ga,
                   wcomb_ref, vec_ref, bias_ref, wo_a, wo_b, pnw_ref, o_ref,
                   c_rkv, c_lora, c_ga, ga_sc, qb_sc, kv_sc, gb_sc, ya_sc, yb_sc, *rwkv_scratch, par):
    ti = pl.program_id(1)
    tc = x_ref.shape[1]
    sc = _RwkvScratch(*rwkv_scratch)
    grp = par * CHUNK
    n_grp = tc // grp
    blocks_per_grp = grp // BLOCK
    bd_mask, head_ones2 = _head_pair_consts()

    @pl.when(ti == 0)
    def _():
        c_rkv[...] = jnp.zeros_like(c_rkv)
        c_lora[...] = jnp.zeros_like(c_lora)
        c_ga[...] = jnp.zeros_like(c_ga)
        sc.s[...] = jnp.zeros_like(sc.s)
        kv_sc[0:BLOCK, :] = jnp.zeros((BLOCK, 2 * B_KV_WIDTH), F32)

    def project(g, proj):
        rows = slice(g * grp, (g + 1) * grp)
        hb = _bf(_rms(x_ref[0, rows, :], nw_ref[...]))

        def shifted(w, mu, carry, lo, hi):
            p = jnp.dot(hb, w[:, lo:hi], preferred_element_type=F32)
            prev = pltpu.roll(p, 1, axis=0)
            prev = jnp.where(_iota(p.shape, 0) == 0, carry[0:1, lo:hi], prev)
            carry[0:1, lo:hi] = p[grp - 1:grp, :]
            return p + mu[:, lo:hi] * (prev - p)

        yield
        for i, name in enumerate(("r", "k", "v")):
            proj[name] = shifted(w_rkv, mu_rkv, c_rkv, i * A_WIDTH, (i + 1) * A_WIDTH)
            yield
        proj["lo"] = shifted(w_lora, mu_lora, c_lora, 0, 2 * A_LORA)
        ga_sc[rows, :] = shifted(w_ga, mu_ga, c_ga, 0, A_WIDTH)
        yield
        qb_sc[rows, :] = _bf(jnp.dot(hb, w_qb[...], preferred_element_type=F32) * (B_HEAD_DIM ** -0.5 * LOG2E))
        kv_sc[BLOCK + g * grp:BLOCK + (g + 1) * grp, :] = jnp.dot(hb, w_kvb[...], preferred_element_type=F32)
        yield
        gb_sc[rows, :] = jnp.dot(hb, w_gb[...], preferred_element_type=F32)
        yield

    project_stages = 7

    def attend(g):
        return _chain(*[_swa_block(j, ti == 0, sink_ref, bias_ref, qb_sc, kv_sc, gb_sc, yb_sc)
                        for j in range(g * blocks_per_grp, (g + 1) * blocks_per_grp)])

    def state(g):
        return _chain(*[_rwkv_state_step(ci, sc, bd_mask) for ci in range(g * par, (g + 1) * par)])

    def finish(g):
        rows = slice(g * grp, (g + 1) * grp)
        yield from _rwkv_finish(g * grp, grp, vec_ref, sc, ga_sc, ya_sc, head_ones2)
        y = (jnp.dot(ya_sc[rows, :], wo_a[...], preferred_element_type=F32)
             + jnp.dot(yb_sc[rows, :], wo_b[...], preferred_element_type=F32))
        yield
        o_ref[0, rows, :] = x_ref[0, rows, :] + _rms(y, pnw_ref[...])
        yield

    finish_stages = 5

    def on(gen_fn, g):
        return gen_fn(g) if 0 <= g < n_grp else None

    projs = [dict() for _ in range(n_grp)]
    _run_interleaved((project(0, projs[0]), project_stages))
    for g in range(n_grp + 3):
        _run_interleaved(
            (project(g + 1, projs[g + 1]) if g + 1 < n_grp else None, project_stages),
            (_rwkv_prep(g * grp, projs[g], wcomb_ref, vec_ref, sc, head_ones2) if g < n_grp else None, PREP_STAGES),
            (_rwkv_local((g - 1) * par, par, sc) if 0 <= g - 1 < n_grp else None, LOCAL_STAGES),
            (on(attend, g - 1), SWA_STAGES * blocks_per_grp),
            (on(state, g - 2), STATE_STAGES * par),
            (on(finish, g - 3), finish_stages))
        if g - 1 == n_grp - 1:
            kv_sc[0:BLOCK, :] = kv_sc[tc:tc + BLOCK, :]


def _layer0(sinks, x, nw, ws, mus, wcomb, vecs, bias, wo_a, wo_b, pnw, tc):
    bsz, t, d = x.shape
    n_chunks = tc // CHUNK
    par = math.gcd(A_PAR, n_chunks)
    full = lambda a: pl.BlockSpec(a.shape, lambda b, i: (0,) * a.ndim)
    tok = pl.BlockSpec((1, tc, d), lambda b, i: (b, i, 0))
    widths = [w.shape[1] for w in ws]
    tile = pltpu.VMEM((tc, A_WIDTH), F32)
    tile16 = pltpu.VMEM((tc, A_WIDTH), BF16)
    scratch = [pltpu.VMEM((8, widths[0]), F32), pltpu.VMEM((8, widths[1]), F32), pltpu.VMEM((8, widths[2]), F32),
               tile,
               tile16,
               pltpu.VMEM((tc + BLOCK, 2 * B_KV_WIDTH), F32),
               tile,
               tile16, tile16,
               pltpu.VMEM((N_PAIRS, LANES, LANES), F32),
               tile, tile, tile, tile, tile,
               pltpu.VMEM((n_chunks * 8, A_WIDTH), F32),
               tile, tile, tile, tile,
               pltpu.VMEM((n_chunks * N_PAIRS, LANES, LANES), F32)]
    return pl.pallas_call(
        functools.partial(_layer0_kernel, par=par), grid=(bsz, t // tc),
        in_specs=[pl.BlockSpec(memory_space=pltpu.SMEM), tok, full(nw)] + [full(w) for w in ws]
                 + [full(m) for m in mus] + [full(wcomb), full(vecs), full(bias), full(wo_a), full(wo_b), full(pnw)],
        out_specs=tok, out_shape=jax.ShapeDtypeStruct((bsz, t, d), F32),
        scratch_shapes=scratch,
        compiler_params=_cparams(("parallel", "arbitrary")), name="layer0",
    )(sinks, x, nw, *ws, *mus, wcomb, vecs, bias, wo_a, wo_b, pnw)


def _layer1_kernel(x_ref, nw_ref, w_ref, lb_ref, cw_ref, wo_ref, pnw_ref, o_ref,
                   s_ref, qd_sc, ki_sc, v_sc, g_sc, dc_sc, o_sc, y_sc, *, par, group):
    ti = pl.program_id(1)

    @pl.when(ti == 0)
    def _():
        s_ref[...] = jnp.zeros_like(s_ref)

    tm = x_ref.shape[1]
    c = CHUNK
    grp = group * c
    n_grp = tm // grp
    l0 = lb_ref[0:1, :]
    l1 = lb_ref[1:2, :]
    m = jnp.maximum(l0, l1)
    e0 = jnp.exp(l0 - m)
    e1 = jnp.exp(l1 - m)
    s0 = e0 / (e0 + e1)
    s1 = e1 / (e0 + e1)
    lb = (s0 + s1) - s0
    amp = 0.5 * (1.0 - lb)
    mid = lb + amp
    causal = _iota((c, c), 1) <= _iota((c, c), 0)
    heads = [slice(hd * C_DIM, (hd + 1) * C_DIM) for hd in range(C_HEADS)]

    def project(g):
        rows = slice(g * grp, (g + 1) * grp)
        p = jnp.dot(_bf(_rms(x_ref[0, rows, :], nw_ref[...])), w_ref[...], preferred_element_type=F32)
        v_sc[rows, :] = _bf(p[:, 2 * C_WIDTH:3 * C_WIDTH])
        g_sc[rows, :] = p[:, 3 * C_WIDTH:4 * C_WIDTH]
        swing = amp * jnp.tanh(0.5 * p[:, C_WIDTH:2 * C_WIDTH])
        gc = _chunk_cumsum(jnp.log2(mid + swing), c)
        qd_sc[rows, :] = _bf(_silu(p[:, 0:C_WIDTH]) * jnp.exp2(gc))
        ki_sc[rows, :] = ((1.0 - mid) - swing) * jnp.exp2(-gc)
        for j in range(group):
            slab = (g * group + j) * 8
            dc_sc[slab:slab + 8, :] = jnp.broadcast_to(jnp.exp2(gc[(j + 1) * c - 1:(j + 1) * c, :]), (8, C_WIDTH))

    def chunks(chunk0, sts):
        chains = [(chunk0 + j, hd) for j in range(par) for hd in range(C_HEADS)]

        def rows(ci):
            return slice(ci * c, (ci + 1) * c)

        q_dec = [qd_sc[rows(ci), heads[hd]] for ci, hd in chains]
        k_inv = [ki_sc[rows(ci), heads[hd]] for ci, hd in chains]
        vv = [v_sc[rows(ci), heads[hd]] for ci, hd in chains]
        dcs = [dc_sc[ci * 8:ci * 8 + 1, heads[hd]] for ci, hd in chains]
        sc = [jnp.where(causal, _mm_nt(qd, ki), 0.0) for qd, ki in zip(q_dec, k_inv)]
        upd = [_mm(vx.astype(F32).T, ki * dc) for vx, ki, dc in zip(vv, k_inv, dcs)]
        intra = [_mm(s, vx) for s, vx in zip(sc, vv)]
        for i, (ci, hd) in enumerate(chains):
            o_sc[rows(ci), heads[hd]] = intra[i] + _mm_nt(q_dec[i], sts[hd])
            sts[hd] = sts[hd] * dcs[i] + upd[i]

    def finish(g):
        rows = slice(g * grp, (g + 1) * grp)
        for sl in heads:
            y_sc[rows, sl] = _bf(_rms(o_sc[rows, sl], cw_ref[...]) * _silu(g_sc[rows, sl]))
        y = jnp.dot(y_sc[rows, :], wo_ref[...], preferred_element_type=F32)
        o_ref[0, rows, :] = x_ref[0, rows, :] + _rms(y, pnw_ref[...])

    sts = [s_ref[hd] for hd in range(C_HEADS)]
    project(0)
    for g in range(n_grp):
        if g + 1 < n_grp:
            project(g + 1)
        for ci in range(g * group, (g + 1) * group, par):
            chunks(ci, sts)
        finish(g)
    for hd in range(C_HEADS):
        s_ref[hd] = sts[hd]


def _layer1(x, nw, w, lbs, cw, wo, pnw, tm):
    bsz, t, d = x.shape
    group = math.gcd(C_GROUP, tm // CHUNK)
    par = math.gcd(C_PAR, group)
    full = lambda a: pl.BlockSpec(a.shape, lambda b, i: (0,) * a.ndim)
    once = lambda a: pl.BlockSpec(a.shape, lambda b, i: (0,) * a.ndim, pipeline_mode=pl.Buffered(1))
    tok = pl.BlockSpec((1, tm, d), lambda b, i: (b, i, 0))
    act16 = pltpu.VMEM((tm, C_WIDTH), BF16)
    act32 = pltpu.VMEM((tm, C_WIDTH), F32)
    return pl.pallas_call(
        functools.partial(_layer1_kernel, par=par, group=group), grid=(bsz, t // tm),
        in_specs=[tok, full(nw), once(w), full(lbs), full(cw), once(wo), full(pnw)],
        out_specs=tok, out_shape=jax.ShapeDtypeStruct((bsz, t, d), F32),
        scratch_shapes=[pltpu.VMEM((C_HEADS, C_DIM, C_DIM), F32),
                        act16, act32, act16, act32,
                        pltpu.VMEM((tm // CHUNK * 8, C_WIDTH), F32),
                        act32, act16],
        compiler_params=_cparams(("parallel", "arbitrary")), name="layer1",
    )(x, nw, w, lbs, cw, wo, pnw)


def _t5_bucket(dist):
    max_exact = REL_BUCKETS // 2
    d = jnp.maximum(dist, 0)
    large = max_exact + jnp.floor(jnp.log(jnp.maximum(d, 1).astype(F32) / max_exact)
                                  / math.log(REL_MAX_DIST / max_exact)
                                  * (REL_BUCKETS - max_exact)).astype(jnp.int32)
    large = jnp.minimum(large, REL_BUCKETS - 1)
    return jnp.where(d < max_exact, d, large)


def _tile(t, want):
    return want if t % want == 0 else BLOCK


def kernel(x, rel_bias_table, lower_bounds, l0_pre_norm, l0_post_norm, l0_w_in, l0_w_out, a_mu, a_w0, a_w2, a_a0, a_a2, a_k_k, a_k_a, a_r_k, a_ln_w, a_ln_b, b_sinks, l1_pre_norm, l1_post_norm, l1_w_in, l1_w_out, c_norm_w):
    bsz, t, d = x.shape
    assert d == D_MODEL and t % BLOCK == 0
    row = lambda v: v.reshape(1, -1).astype(F32)

    c_rkv, c_lora, c_ga = 3 * A_WIDTH, 3 * A_WIDTH + 2 * A_LORA, 4 * A_WIDTH + 2 * A_LORA
    c_q, c_kv, c_gb = c_ga + B_WIDTH, c_ga + B_WIDTH + 2 * B_KV_WIDTH, c_ga + 2 * B_WIDTH + 2 * B_KV_WIDTH
    wb = _bf(l0_w_in)
    ws = [wb[:, 0:c_rkv], wb[:, c_rkv:c_lora], wb[:, c_lora:c_ga], wb[:, c_ga:c_q], wb[:, c_q:c_kv], wb[:, c_kv:c_gb]]
    mus = [row(a_mu[0:c_rkv]), row(a_mu[c_rkv:c_lora]), row(a_mu[c_lora:c_ga])]
    zeros = jnp.zeros((A_LORA, A_WIDTH), F32)
    wcomb = jnp.concatenate([jnp.concatenate([a_w2, zeros], axis=1), jnp.concatenate([zeros, a_a2], axis=1)], axis=0)
    vecs = jnp.stack([a_w0, a_a0, a_k_k, a_k_a, a_r_k.reshape(-1), a_ln_w, a_ln_b, jnp.zeros_like(a_w0)]).astype(F32)
    qi = jnp.arange(BLOCK)[:, None]
    kj = jnp.arange(2 * BLOCK)[None, :]
    dist = qi + BLOCK - kj
    onehot = (_t5_bucket(dist)[:, :, None] == jnp.arange(REL_BUCKETS)).astype(F32)
    bias = jnp.einsum("qkb,bh->hqk", onehot, rel_bias_table.astype(F32), precision=lax.Precision.HIGHEST)
    in_window = (dist >= 0) & (dist < WINDOW)
    bias = bias * LOG2E
    bias = jnp.stack([jnp.where(in_window & (kj >= BLOCK), bias, NEG_INF), jnp.where(in_window, bias, NEG_INF)])
    w_out0 = _bf(l0_w_out)
    w_in1 = _bf(l1_w_in)
    w_out1 = _bf(l1_w_out)

    x1 = _layer0(b_sinks.astype(F32) * LOG2E, x, row(l0_pre_norm), ws, mus, wcomb, vecs, bias,
                 w_out0[0:A_WIDTH], w_out0[A_WIDTH:], row(l0_post_norm), _tile(t, 512))
    return _layer1(x1, row(l1_pre_norm), w_in1, lower_bounds.astype(F32), row(c_norm_w), w_out1, row(l1_post_norm),
                   _tile(t, 1024))
```

```python
import functools
import math

import jax
import jax.numpy as jnp
from jax import lax
from jax.experimental import pallas as pl
from jax.experimental.pallas import tpu as pltpu

D_MODEL = 1024
A_HEAD_DIM = 64
A_WIDTH = 512
A_LORA = 32
A_GN_EPS = 64e-5
B_HEAD_DIM = 64
B_WIDTH = 512
B_HEADS = 8
B_KV_WIDTH = 128
B_GROUP = 4
WINDOW = 128
BLOCK = 128
REL_BUCKETS = 32
REL_MAX_DIST = 128
C_HEADS = 8
C_DIM = 128
C_WIDTH = 1024
CHUNK = 64
NORM_EPS = 1e-6
NEG_INF = -1e30
LOG2E = math.log2(math.e)

LANES = 128
MXU_DEPTH = 256
N_PAIRS = A_WIDTH // LANES
A_PAR = 4
C_PAR = 2
C_GROUP = 4
VMEM_LIMIT = 61 * 1024 * 1024

F32 = jnp.float32
BF16 = jnp.bfloat16


def _bf(x):
    return x.astype(BF16)


def _mm(a, b):
    return jnp.dot(_bf(a), _bf(b), preferred_element_type=F32)


def _mm_nt(a, b):
    return lax.dot_general(_bf(a), _bf(b), (((1,), (1,)), ((), ())), preferred_element_type=F32)


def _split(x, parts):
    out = []
    rem = x
    for i in range(parts):
        hi = _bf(rem)
        out.append(hi)
        if i + 1 < parts:
            rem = rem - hi.astype(F32)
    return out


def _mm_exact_lhs(a_bf, x, parts):
    n = x.shape[1]
    xs = jnp.concatenate(_split(x, parts), axis=1)
    y = jnp.dot(a_bf, xs, preferred_element_type=F32)
    acc = y[:, 0:n]
    for i in range(1, parts):
        acc = acc + y[:, i * n:(i + 1) * n]
    return acc


def _mm_exact_rhs(x, b_stacked, parts):
    return jnp.dot(jnp.concatenate(_split(x, parts), axis=1), b_stacked, preferred_element_type=F32)


def _mm3(a, b):
    a_hi, a_lo = _split(a, 2)
    b_hi, b_lo = _split(b, 2)
    return jnp.dot(jnp.concatenate([a_hi, a_hi, a_lo], axis=1), jnp.concatenate([b_hi, b_lo, b_hi], axis=0),
                   preferred_element_type=F32)


def _iota(shape, dim):
    return lax.broadcasted_iota(jnp.int32, shape, dim)


def _sigmoid(x):
    return 0.5 * (1.0 + jnp.tanh(0.5 * x))


def _silu(x):
    return x * _sigmoid(x)


def _chunk_cumsum(x, c):
    t = x.shape[0]
    blk = min(t, MXU_DEPTH)
    i = _iota((blk, blk), 0)
    j = _iota((blk, blk), 1)
    cum = jnp.where(((i // c) == (j // c)) & (j <= i), 1.0, 0.0).astype(BF16)
    return jnp.concatenate([_mm_exact_lhs(cum, x[b * blk:(b + 1) * blk], 2) for b in range(t // blk)], axis=0)


def _rms(x, w):
    return x * lax.rsqrt(jnp.mean(x * x, axis=-1, keepdims=True) + NORM_EPS) * w


def _cparams(sem):
    return pltpu.CompilerParams(dimension_semantics=sem, vmem_limit_bytes=VMEM_LIMIT)


def _chain(*gens):
    for gen in gens:
        yield from gen


def _run_interleaved(*stages):
    live = [[gen, max(n, 1), 0] for gen, n in stages if gen is not None]
    ticks = max((n for _, n, _ in live), default=0)
    for t in range(ticks + 1):
        for item in live:
            gen, n, done = item
            want = n if t == ticks else (t + 1) * n // ticks
            while gen is not None and (done < want or t == ticks):
                try:
                    next(gen)
                    done += 1
                except StopIteration:
                    gen = None
            item[0], item[2] = gen, done


def _bd(x):
    h0 = _iota(x.shape, 1) < A_HEAD_DIM
    return jnp.concatenate([jnp.where(h0, x, 0.0), jnp.where(h0, 0.0, x)], axis=0)


def _unit_lower_inverse(ns, out):
    c = ns[0].shape[0]
    row = _iota(ns[0].shape, 0)
    col = _iota(ns[0].shape, 1) & (A_HEAD_DIM - 1)
    eye = jnp.where(col == row, 1.0, 0.0)
    first = (col == row - 1) & ((row & 1) == 1)
    ts = [eye + jnp.where(first, n, 0.0) for n in ns]
    bs = 2
    while bs < c:
        sel = ((row // (2 * bs)) == (col // (2 * bs))) & (((row // bs) & 1) == 1) & (((col // bs) & 1) == 0)
        xs = [_mm(jnp.where(sel, n, 0.0), _bd(t)) for n, t in zip(ns, ts)]
        yield
        ts = [t + _mm(t, _bd(x)) for t, x in zip(ts, xs)]
        yield
        bs *= 2
    out[:] = ts


class _RwkvScratch:
    def __init__(self, s, v, ad, rd, bi, ki, dc, q, y, w, bo, uv):
        self.s, self.v, self.ad, self.rd, self.bi, self.ki, self.dc = s, v, ad, rd, bi, ki, dc
        self.q, self.y, self.w, self.bo, self.uv = q, y, w, bo, uv


def _head_pair_consts():
    li = _iota((LANES, LANES), 0) // A_HEAD_DIM
    lj = _iota((LANES, LANES), 1) // A_HEAD_DIM
    bd_mask = li == lj
    head_ones = jnp.where(bd_mask, 1.0, 0.0).astype(BF16)
    return bd_mask, jnp.concatenate([head_ones, head_ones], axis=0)


_PAIRS = [slice(p * LANES, (p + 1) * LANES) for p in range(N_PAIRS)]


def _headsum(x, head_ones2):
    return jnp.concatenate([_mm_exact_rhs(x[:, sl], head_ones2, 2) for sl in _PAIRS], axis=1)


def _rwkv_prep(r0, proj, wcomb_ref, vec_ref, sc, head_ones2):
    r, k, v, lo = proj["r"], proj["k"], proj["v"], proj["lo"]
    n = r.shape[0]
    c = CHUNK
    rows = slice(r0, r0 + n)
    w0, a0, k_k, k_a, r_k = (vec_ref[i:i + 1, :] for i in range(5))
    z = jnp.where(_iota(lo.shape, 1) < A_LORA, jnp.tanh(lo), lo)
    wa = _mm3(z, wcomb_ref[...])
    logw = -math.exp(-0.5) * _sigmoid(w0 + wa[:, 0:A_WIDTH])
    yield
    g = _chunk_cumsum(logw, c)
    a = _sigmoid(a0 + wa[:, A_WIDTH:2 * A_WIDTH])
    yield
    kk = k * k_k
    kkn = kk * lax.rsqrt(jnp.maximum(_headsum(kk * kk, head_ones2), 1e-24))
    yield
    k2 = k * (1.0 + (a - 1.0) * k_a)
    sc.v[rows, :] = v
    sc.bo[rows, :] = _headsum(r * k2 * r_k, head_ones2) * v
    yield
    e_neg = jnp.exp(-g)
    sc.bi[rows, :] = kkn * a * e_neg
    sc.ki[rows, :] = k2 * e_neg
    yield
    sc.ad[rows, :] = -kkn * jnp.exp(g - logw)
    sc.rd[rows, :] = r * jnp.exp(g)
    for j in range(n // c):
        slab = (r0 // c + j) * 8
        sc.dc[slab:slab + 8, :] = jnp.broadcast_to(jnp.exp(g[(j + 1) * c - 1:(j + 1) * c, :]), (8, A_WIDTH))
    yield


PREP_STAGES = 6


def _rwkv_local(chunk0, par, sc):
    c = CHUNK
    prow = _iota((c, LANES), 0)
    pcol = _iota((c, LANES), 1) & (A_HEAD_DIM - 1)
    strict = pcol < prow
    incl = pcol <= prow
    head0 = _iota((c, LANES), 1) < A_HEAD_DIM
    chains = [(chunk0 + j, p) for j in range(par) for p in range(N_PAIRS)]

    def rows(ci):
        return slice(ci * c, (ci + 1) * c)

    a_dec = [sc.ad[rows(ci), _PAIRS[p]] for ci, p in chains]
    r_dec = [sc.rd[rows(ci), _PAIRS[p]] for ci, p in chains]
    b_inv = [sc.bi[rows(ci), _PAIRS[p]] for ci, p in chains]
    k_inv = [sc.ki[rows(ci), _PAIRS[p]] for ci, p in chains]
    vv = [sc.v[rows(ci), _PAIRS[p]] for ci, p in chains]

    gms = [_mm_nt(jnp.concatenate([ad, rd], axis=0),
                  jnp.concatenate([jnp.where(head0, bp, 0.0), jnp.where(head0, 0.0, bp),
                                   jnp.where(head0, kp, 0.0), jnp.where(head0, 0.0, kp)], axis=0))
           for ad, rd, bp, kp in zip(a_dec, r_dec, b_inv, k_inv)]
    yield
    n_ab = [jnp.where(strict, gm[0:c, 0:LANES], 0.0) for gm in gms]
    a_ak = [jnp.where(strict, gm[0:c, LANES:2 * LANES], 0.0) for gm in gms]
    a_rb = [jnp.where(incl, gm[c:2 * c, 0:LANES], 0.0) for gm in gms]
    a_rk = [jnp.where(incl, gm[c:2 * c, LANES:2 * LANES], 0.0) for gm in gms]
    tinv = []
    yield from _unit_lower_inverse(n_ab, tinv)
    from_v = [_mm(jnp.concatenate([ak, rk], axis=0), _bd(x)) for ak, rk, x in zip(a_ak, a_rk, vv)]
    yield
    tw = [_mm(t, jnp.concatenate([_bd(ad), _bd(fv[0:c])], axis=1))
          for t, fv, ad in zip(tinv, from_v, a_dec)]
    yield
    ab = [_mm(rb, jnp.concatenate([_bd(x[:, 0:LANES]), _bd(x[:, LANES:2 * LANES])], axis=1))
          for rb, x in zip(a_rb, tw)]
    yield
    for i, (ci, p) in enumerate(chains):
        slot = ci * N_PAIRS + p
        sc.w[rows(ci), _PAIRS[p]] = tw[i][:, 0:LANES]
        sc.q[rows(ci), _PAIRS[p]] = r_dec[i] + ab[i][:, 0:LANES]
        sc.y[rows(ci), _PAIRS[p]] = ab[i][:, LANES:2 * LANES] + from_v[i][c:2 * c]
        sc.uv[slot] = jnp.concatenate([tw[i][:, LANES:2 * LANES], vv[i]], axis=0).T
    yield


LOCAL_STAGES = 15


def _rwkv_state_step(ci, sc, bd_mask):
    c = CHUNK
    rows = slice(ci * c, (ci + 1) * c)
    dc = sc.dc[ci * 8:ci * 8 + 1, :]
    sts = [sc.s[p] for p in range(N_PAIRS)]
    zeros = jnp.zeros((c, LANES), F32)
    uvs = [_mm_nt(st, jnp.concatenate([sc.w[rows, sl], zeros], axis=0)) + sc.uv[ci * N_PAIRS + p]
           for p, (st, sl) in enumerate(zip(sts, _PAIRS))]
    ys = [_mm_nt(sc.q[rows, sl], st) for st, sl in zip(sts, _PAIRS)]
    yield
    upd = [_mm(uv, jnp.concatenate([sc.bi[rows, sl], sc.ki[rows, sl]], axis=0) * dc[:, sl])
           for uv, sl in zip(uvs, _PAIRS)]
    yield
    for p, sl in enumerate(_PAIRS):
        sc.s[p] = sts[p] * dc[:, sl] + jnp.where(bd_mask, upd[p], 0.0)
        sc.y[rows, sl] = sc.y[rows, sl] + ys[p]
    yield


STATE_STAGES = 3


def _rwkv_finish(r0, n, vec_ref, sc, ga_sc, ya_sc, head_ones2):
    rows = slice(r0, r0 + n)
    ln_w = vec_ref[5:6, :]
    ln_b = vec_ref[6:7, :]
    y = sc.y[rows, :]
    inv_n = 1.0 / A_HEAD_DIM
    mean = _headsum(y, head_ones2) * inv_n
    yield
    d = y - mean
    var = _headsum(d * d, head_ones2) * inv_n
    yield
    yn = d * lax.rsqrt(var + A_GN_EPS) * ln_w + ln_b
    ya_sc[rows, :] = _bf((yn + sc.bo[rows, :]) * _silu(ga_sc[rows, :]))
    yield


def _swa_block(j, first_tile, sink_ref, bias_ref, qb_sc, kv_sc, gb_sc, yb_sc):
    h0 = _iota((2 * BLOCK, B_KV_WIDTH), 1) < B_HEAD_DIM
    heads = [(hq // 2, hq % 2, hq // B_GROUP) for hq in range(B_HEADS)]

    def ext(straight, rolled, kvh, e):
        src = straight if kvh == e else rolled
        return _bf(jnp.where(h0, src, 0.0) if e == 0 else jnp.where(h0, 0.0, src))

    rows = slice(j * BLOCK, (j + 1) * BLOCK)
    variant = jnp.where(first_tile, 0, 1) if j == 0 else 1
    kv = kv_sc[j * BLOCK:(j + 2) * BLOCK, :]
    kc = kv[:, 0:B_KV_WIDTH]
    vc = kv[:, B_KV_WIDTH:2 * B_KV_WIDTH]
    kr = pltpu.roll(kc, B_HEAD_DIM, axis=1)
    vr = pltpu.roll(vc, B_HEAD_DIM, axis=1)
    kext = {(kvh, e): ext(kc, kr, kvh, e) for _, e, kvh in heads}
    vext = {(kvh, e): ext(vc, vr, kvh, e) for _, e, kvh in heads}
    q = [qb_sc[rows, p * LANES:(p + 1) * LANES] for p in range(B_WIDTH // LANES)]
    yield
    out = [None] * B_HEADS
    for h_lo in range(0, B_HEADS, 2):
        grp = list(enumerate(heads))[h_lo:h_lo + 2]
        s = [lax.dot_general(q[p], kext[(kvh, e)], (((1,), (1,)), ((), ())), preferred_element_type=F32)
             + bias_ref[variant, hq] for hq, (p, e, kvh) in grp]
        m = [jnp.maximum(jnp.max(x, axis=-1, keepdims=True), sink_ref[hq]) for (hq, _), x in zip(grp, s)]
        yield
        pe = [jnp.exp2(x - mx) for x, mx in zip(s, m)]
        inv = [1.0 / (jnp.sum(x, axis=-1, keepdims=True) + jnp.exp2(sink_ref[hq] - mx))
               for (hq, _), x, mx in zip(grp, pe, m)]
        yield
        for (hq, (_, e, kvh)), x, rcp in zip(grp, pe, inv):
            out[hq] = jnp.dot(_bf(x), vext[(kvh, e)], preferred_element_type=F32) * rcp
        p = h_lo // 2
        sl = slice(p * LANES, (p + 1) * LANES)
        yb_sc[rows, sl] = _bf((out[h_lo] + out[h_lo + 1]) * _silu(gb_sc[rows, sl]))
        yield


SWA_STAGES = 13


def _layer0_kernel(sink_ref, x_ref, nw_ref, w_rkv, w_lora, w_ga, w_qb, w_kvb, w_gb, mu_rkv, mu_lora, mu_ga,
                   wcomb_ref, vec_ref, bias_ref, wo_a, wo_b, pnw_ref, o_ref,
                   c_rkv, c_lora, c_ga, ga_sc, qb_sc, kv_sc, gb_sc, ya_sc, yb_sc, *rwkv_scratch, par):
    ti = pl.program_id(1)
    tc = x_ref.shape[1]
    sc = _RwkvScratch(*rwkv_scratch)
    grp = par * CHUNK
    n_grp = tc // grp
    blocks_per_grp = grp // BLOCK
    bd_mask, head_ones2 = _head_pair_consts()

    @pl.when(ti == 0)
    def _():
        c_rkv[...] = jnp.zeros_like(c_rkv)
        c_lora[...] = jnp.zeros_like(c_lora)
        c_ga[...] = jnp.zeros_like(c_ga)
        sc.s[...] = jnp.zeros_like(sc.s)
        kv_sc[0:BLOCK, :] = jnp.zeros((BLOCK, 2 * B_KV_WIDTH), F32)

    def project(g, proj):
        rows = slice(g * grp, (g + 1) * grp)
        hb = _bf(_rms(x_ref[0, rows, :], nw_ref[...]))

        def shifted(w, mu, carry, lo, hi):
            p = jnp.dot(hb, w[:, lo:hi], preferred_element_type=F32)
            prev = pltpu.roll(p, 1, axis=0)
            prev = jnp.where(_iota(p.shape, 0) == 0, carry[0:1, lo:hi], prev)
            carry[0:1, lo:hi] = p[grp - 1:grp, :]
            return p + mu[:, lo:hi] * (prev - p)

        yield
        for i, name in enumerate(("r", "k", "v")):
            proj[name] = shifted(w_rkv, mu_rkv, c_rkv, i * A_WIDTH, (i + 1) * A_WIDTH)
            yield
        proj["lo"] = shifted(w_lora, mu_lora, c_lora, 0, 2 * A_LORA)
        ga_sc[rows, :] = shifted(w_ga, mu_ga, c_ga, 0, A_WIDTH)
        yield
        qb_sc[rows, :] = _bf(jnp.dot(hb, w_qb[...], preferred_element_type=F32) * (B_HEAD_DIM ** -0.5 * LOG2E))
        kv_sc[BLOCK + g * grp:BLOCK + (g + 1) * grp, :] = jnp.dot(hb, w_kvb[...], preferred_element_type=F32)
        yield
        gb_sc[rows, :] = jnp.dot(hb, w_gb[...], preferred_element_type=F32)
        yield

    project_stages = 7

    def attend(g):
        return _chain(*[_swa_block(j, ti == 0, sink_ref, bias_ref, qb_sc, kv_sc, gb_sc, yb_sc)
                        for j in range(g * blocks_per_grp, (g + 1) * blocks_per_grp)])

    def state(g):
        return _chain(*[_rwkv_state_step(ci, sc, bd_mask) for ci in range(g * par, (g + 1) * par)])

    def finish(g):
        rows = slice(g * grp, (g + 1) * grp)
        yield from _rwkv_finish(g * grp, grp, vec_ref, sc, ga_sc, ya_sc, head_ones2)
        y = (jnp.dot(ya_sc[rows, :], wo_a[...], preferred_element_type=F32)
             + jnp.dot(yb_sc[rows, :], wo_b[...], preferred_element_type=F32))
        yield
        o_ref[0, rows, :] = x_ref[0, rows, :] + _rms(y, pnw_ref[...])
        yield

    finish_stages = 5

    def on(gen_fn, g):
        return gen_fn(g) if 0 <= g < n_grp else None

    projs = [dict() for _ in range(n_grp)]
    _run_interleaved((project(0, projs[0]), project_stages))
    for g in range(n_grp + 3):
        _run_interleaved(
            (project(g + 1, projs[g + 1]) if g + 1 < n_grp else None, project_stages),
            (_rwkv_prep(g * grp, projs[g], wcomb_ref, vec_ref, sc, head_ones2) if g < n_grp else None, PREP_STAGES),
            (_rwkv_local((g - 1) * par, par, sc) if 0 <= g - 1 < n_grp else None, LOCAL_STAGES),
            (on(attend, g - 1), SWA_STAGES * blocks_per_grp),
            (on(state, g - 2), STATE_STAGES * par),
            (on(finish, g - 3), finish_stages))
        if g - 1 == n_grp - 1:
            kv_sc[0:BLOCK, :] = kv_sc[tc:tc + BLOCK, :]


def _layer0(sinks, x, nw, ws, mus, wcomb, vecs, bias, wo_a, wo_b, pnw, tc):
    bsz, t, d = x.shape
    n_chunks = tc // CHUNK
    par = math.gcd(A_PAR, n_chunks)
    full = lambda a: pl.BlockSpec(a.shape, lambda b, i: (0,) * a.ndim)
    once = lambda a: pl.BlockSpec(a.shape, lambda b, i: (0,) * a.ndim, pipeline_mode=pl.Buffered(1))
    tok = pl.BlockSpec((1, tc, d), lambda b, i: (b, i, 0))
    widths = [w.shape[1] for w in ws]
    tile = pltpu.VMEM((tc, A_WIDTH), F32)
    tile16 = pltpu.VMEM((tc, A_WIDTH), BF16)
    scratch = [pltpu.VMEM((8, widths[0]), F32), pltpu.VMEM((8, widths[1]), F32), pltpu.VMEM((8, widths[2]), F32),
               tile,
               tile16,
               pltpu.VMEM((tc + BLOCK, 2 * B_KV_WIDTH), F32),
               tile,
               tile16, tile16,
               pltpu.VMEM((N_PAIRS, LANES, LANES), F32),
               tile, tile, tile, tile, tile,
               pltpu.VMEM((n_chunks * 8, A_WIDTH), F32),
               tile, tile, tile, tile,
               pltpu.VMEM((n_chunks * N_PAIRS, LANES, LANES), F32)]
    return pl.pallas_call(
        functools.partial(_layer0_kernel, par=par), grid=(bsz, t // tc),
        in_specs=[pl.BlockSpec(memory_space=pltpu.SMEM), tok, full(nw)] + [once(w) for w in ws]
                 + [full(m) for m in mus] + [full(wcomb), full(vecs), once(bias), once(wo_a), once(wo_b), full(pnw)],
        out_specs=tok, out_shape=jax.ShapeDtypeStruct((bsz, t, d), F32),
        scratch_shapes=scratch,
        compiler_params=_cparams(("parallel", "arbitrary")), name="layer0",
    )(sinks, x, nw, *ws, *mus, wcomb, vecs, bias, wo_a, wo_b, pnw)


def _layer1_kernel(x_ref, nw_ref, w_ref, lb_ref, cw_ref, wo_ref, pnw_ref, o_ref,
                   s_ref, qd_sc, ki_sc, v_sc, g_sc, dc_sc, o_sc, y_sc, *, par, group):
    ti = pl.program_id(1)

    @pl.when(ti == 0)
    def _():
        s_ref[...] = jnp.zeros_like(s_ref)

    tm = x_ref.shape[1]
    c = CHUNK
    grp = group * c
    n_grp = tm // grp
    l0 = lb_ref[0:1, :]
    l1 = lb_ref[1:2, :]
    m = jnp.maximum(l0, l1)
    e0 = jnp.exp(l0 - m)
    e1 = jnp.exp(l1 - m)
    s0 = e0 / (e0 + e1)
    s1 = e1 / (e0 + e1)
    lb = (s0 + s1) - s0
    causal = _iota((c, c), 1) <= _iota((c, c), 0)
    heads = [slice(hd * C_DIM, (hd + 1) * C_DIM) for hd in range(C_HEADS)]

    def project(g):
        rows = slice(g * grp, (g + 1) * grp)
        p = jnp.dot(_bf(_rms(x_ref[0, rows, :], nw_ref[...])), w_ref[...], preferred_element_type=F32)
        v_sc[rows, :] = _bf(p[:, 2 * C_WIDTH:3 * C_WIDTH])
        g_sc[rows, :] = p[:, 3 * C_WIDTH:4 * C_WIDTH]
        forget = lb + (1.0 - lb) * _sigmoid(p[:, C_WIDTH:2 * C_WIDTH])
        gc = _chunk_cumsum(jnp.log(forget), c)
        qd_sc[rows, :] = _bf(_silu(p[:, 0:C_WIDTH]) * jnp.exp(gc))
        ki_sc[rows, :] = (1.0 - forget) * jnp.exp(-gc)
        for j in range(group):
            slab = (g * group + j) * 8
            dc_sc[slab:slab + 8, :] = jnp.broadcast_to(jnp.exp(gc[(j + 1) * c - 1:(j + 1) * c, :]), (8, C_WIDTH))

    def chunks(chunk0, sts):
        chains = [(chunk0 + j, hd) for j in range(par) for hd in range(C_HEADS)]

        def rows(ci):
            return slice(ci * c, (ci + 1) * c)

        q_dec = [qd_sc[rows(ci), heads[hd]] for ci, hd in chains]
        k_inv = [ki_sc[rows(ci), heads[hd]] for ci, hd in chains]
        vv = [v_sc[rows(ci), heads[hd]] for ci, hd in chains]
        dcs = [dc_sc[ci * 8:ci * 8 + 1, heads[hd]] for ci, hd in chains]
        sc = [jnp.where(causal, _mm_nt(qd, ki), 0.0) for qd, ki in zip(q_dec, k_inv)]
        upd = [_mm(vx.astype(F32).T, ki * dc) for vx, ki, dc in zip(vv, k_inv, dcs)]
        intra = [_mm(s, vx) for s, vx in zip(sc, vv)]
        for i, (ci, hd) in enumerate(chains):
            o_sc[rows(ci), heads[hd]] = intra[i] + _mm_nt(q_dec[i], sts[hd])
            sts[hd] = sts[hd] * dcs[i] + upd[i]

    def finish(g):
        rows = slice(g * grp, (g + 1) * grp)
        for sl in heads:
            y_sc[rows, sl] = _bf(_rms(o_sc[rows, sl], cw_ref[...]) * _silu(g_sc[rows, sl]))
        y = jnp.dot(y_sc[rows, :], wo_ref[...], preferred_element_type=F32)
        o_ref[0, rows, :] = x_ref[0, rows, :] + _rms(y, pnw_ref[...])

    sts = [s_ref[hd] for hd in range(C_HEADS)]
    project(0)
    for g in range(n_grp):
        if g + 1 < n_grp:
            project(g + 1)
        for ci in range(g * group, (g + 1) * group, par):
            chunks(ci, sts)
        finish(g)
    for hd in range(C_HEADS):
        s_ref[hd] = sts[hd]


def _layer1(x, nw, w, lbs, cw, wo, pnw, tm):
    bsz, t, d = x.shape
    group = math.gcd(C_GROUP, tm // CHUNK)
    par = math.gcd(C_PAR, group)
    full = lambda a: pl.BlockSpec(a.shape, lambda b, i: (0,) * a.ndim)
    once = lambda a: pl.BlockSpec(a.shape, lambda b, i: (0,) * a.ndim, pipeline_mode=pl.Buffered(1))
    tok = pl.BlockSpec((1, tm, d), lambda b, i: (b, i, 0))
    act16 = pltpu.VMEM((tm, C_WIDTH), BF16)
    act32 = pltpu.VMEM((tm, C_WIDTH), F32)
    return pl.pallas_call(
        functools.partial(_layer1_kernel, par=par, group=group), grid=(bsz, t // tm),
        in_specs=[tok, full(nw), once(w), full(lbs), full(cw), once(wo), full(pnw)],
        out_specs=tok, out_shape=jax.ShapeDtypeStruct((bsz, t, d), F32),
        scratch_shapes=[pltpu.VMEM((C_HEADS, C_DIM, C_DIM), F32),
                        act16, act32, act16, act32,
                        pltpu.VMEM((tm // CHUNK * 8, C_WIDTH), F32),
                        act32, act16],
        compiler_params=_cparams(("parallel", "arbitrary")), name="layer1",
    )(x, nw, w, lbs, cw, wo, pnw)


def _t5_bucket(dist):
    max_exact = REL_BUCKETS // 2
    d = jnp.maximum(dist, 0)
    large = max_exact + jnp.floor(jnp.log(jnp.maximum(d, 1).astype(F32) / max_exact)
                                  / math.log(REL_MAX_DIST / max_exact)
                                  * (REL_BUCKETS - max_exact)).astype(jnp.int32)
    large = jnp.minimum(large, REL_BUCKETS - 1)
    return jnp.where(d < max_exact, d, large)


def _tile(t, want):
    return want if t % want == 0 else BLOCK


def kernel(x, rel_bias_table, lower_bounds, l0_pre_norm, l0_post_norm, l0_w_in, l0_w_out, a_mu, a_w0, a_w2, a_a0, a_a2, a_k_k, a_k_a, a_r_k, a_ln_w, a_ln_b, b_sinks, l1_pre_norm, l1_post_norm, l1_w_in, l1_w_out, c_norm_w):
    bsz, t, d = x.shape
    assert d == D_MODEL and t % BLOCK == 0
    row = lambda v: v.reshape(1, -1).astype(F32)

    c_rkv, c_lora, c_ga = 3 * A_WIDTH, 3 * A_WIDTH + 2 * A_LORA, 4 * A_WIDTH + 2 * A_LORA
    c_q, c_kv, c_gb = c_ga + B_WIDTH, c_ga + B_WIDTH + 2 * B_KV_WIDTH, c_ga + 2 * B_WIDTH + 2 * B_KV_WIDTH
    wb = _bf(l0_w_in)
    ws = [wb[:, 0:c_rkv], wb[:, c_rkv:c_lora], wb[:, c_lora:c_ga], wb[:, c_ga:c_q], wb[:, c_q:c_kv], wb[:, c_kv:c_gb]]
    mus = [row(a_mu[0:c_rkv]), row(a_mu[c_rkv:c_lora]), row(a_mu[c_lora:c_ga])]
    zeros = jnp.zeros((A_LORA, A_WIDTH), F32)
    wcomb = jnp.concatenate([jnp.concatenate([a_w2, zeros], axis=1), jnp.concatenate([zeros, a_a2], axis=1)], axis=0)
    vecs = jnp.stack([a_w0, a_a0, a_k_k, a_k_a, a_r_k.reshape(-1), a_ln_w, a_ln_b, jnp.zeros_like(a_w0)]).astype(F32)
    qi = jnp.arange(BLOCK)[:, None]
    kj = jnp.arange(2 * BLOCK)[None, :]
    dist = qi + BLOCK - kj
    onehot = (_t5_bucket(dist)[:, :, None] == jnp.arange(REL_BUCKETS)).astype(F32)
    bias = jnp.einsum("qkb,bh->hqk", onehot, rel_bias_table.astype(F32), precision=lax.Precision.HIGHEST)
    in_window = (dist >= 0) & (dist < WINDOW)
    bias = bias * LOG2E
    bias = jnp.stack([jnp.where(in_window & (kj >= BLOCK), bias, NEG_INF), jnp.where(in_window, bias, NEG_INF)])
    w_out0 = _bf(l0_w_out)
    w_in1 = _bf(l1_w_in)
    w_out1 = _bf(l1_w_out)

    x1 = _layer0(b_sinks.astype(F32) * LOG2E, x, row(l0_pre_norm), ws, mus, wcomb, vecs, bias,
                 w_out0[0:A_WIDTH], w_out0[A_WIDTH:], row(l0_post_norm), _tile(t, 1024))
    return _layer1(x1, row(l1_pre_norm), w_in1, lower_bounds.astype(F32), row(c_norm_w), w_out1, row(l1_post_norm),
                   _tile(t, 1024))
```

```python
import functools
import math

import jax
import jax.numpy as jnp
from jax import lax
from jax.experimental import pallas as pl
from jax.experimental.pallas import tpu as pltpu

D_MODEL = 1024
A_HEAD_DIM = 64
A_WIDTH = 512
A_LORA = 32
A_GN_EPS = 64e-5
B_HEAD_DIM = 64
B_WIDTH = 512
B_HEADS = 8
B_KV_WIDTH = 128
B_GROUP = 4
WINDOW = 128
BLOCK = 128
REL_BUCKETS = 32
REL_MAX_DIST = 128
C_HEADS = 8
C_DIM = 128
C_WIDTH = 1024
CHUNK = 64
NORM_EPS = 1e-6
NEG_INF = -1e30
LOG2E = math.log2(math.e)

LANES = 128
MXU_DEPTH = 256
N_PAIRS = A_WIDTH // LANES
A_PAR = 4
C_PAR = 4
C_GROUP = 4
VMEM_LIMIT = 56 * 1024 * 1024

F32 = jnp.float32
BF16 = jnp.bfloat16


def _bf(x):
    return x.astype(BF16)


def _mm(a, b):
    return jnp.dot(_bf(a), _bf(b), preferred_element_type=F32)


def _mm_nt(a, b):
    return lax.dot_general(_bf(a), _bf(b), (((1,), (1,)), ((), ())), preferred_element_type=F32)


def _split(x, parts):
    out = []
    rem = x
    for i in range(parts):
        hi = _bf(rem)
        out.append(hi)
        if i + 1 < parts:
            rem = rem - hi.astype(F32)
    return out


def _mm_exact_lhs(a_bf, x, parts):
    n = x.shape[1]
    xs = jnp.concatenate(_split(x, parts), axis=1)
    y = jnp.dot(a_bf, xs, preferred_element_type=F32)
    acc = y[:, 0:n]
    for i in range(1, parts):
        acc = acc + y[:, i * n:(i + 1) * n]
    return acc


def _mm_exact_rhs(x, b_stacked, parts):
    return jnp.dot(jnp.concatenate(_split(x, parts), axis=1), b_stacked, preferred_element_type=F32)


def _mm3(a, b):
    a_hi, a_lo = _split(a, 2)
    b_hi, b_lo = _split(b, 2)
    return jnp.dot(jnp.concatenate([a_hi, a_hi, a_lo], axis=1), jnp.concatenate([b_hi, b_lo, b_hi], axis=0),
                   preferred_element_type=F32)


def _iota(shape, dim):
    return lax.broadcasted_iota(jnp.int32, shape, dim)


def _sigmoid(x):
    return 0.5 * (1.0 + jnp.tanh(0.5 * x))


def _silu(x):
    return x * _sigmoid(x)


def _chunk_cumsum(x, c):
    t = x.shape[0]
    blk = min(t, MXU_DEPTH)
    i = _iota((blk, blk), 0)
    j = _iota((blk, blk), 1)
    cum = jnp.where(((i // c) == (j // c)) & (j <= i), 1.0, 0.0).astype(BF16)
    return jnp.concatenate([_mm_exact_lhs(cum, x[b * blk:(b + 1) * blk], 2) for b in range(t // blk)], axis=0)


def _rms(x, w):
    return x * lax.rsqrt(jnp.mean(x * x, axis=-1, keepdims=True) + NORM_EPS) * w


def _cparams(sem):
    return pltpu.CompilerParams(dimension_semantics=sem, vmem_limit_bytes=VMEM_LIMIT)


def _chain(*gens):
    for gen in gens:
        yield from gen


def _run_interleaved(*stages):
    live = [[gen, max(n, 1), 0] for gen, n in stages if gen is not None]
    ticks = max((n for _, n, _ in live), default=0)
    for t in range(ticks + 1):
        for item in live:
            gen, n, done = item
            want = n if t == ticks else (t + 1) * n // ticks
            while gen is not None and (done < want or t == ticks):
                try:
                    next(gen)
                    done += 1
                except StopIteration:
                    gen = None
            item[0], item[2] = gen, done


def _bd(x):
    h0 = _iota(x.shape, 1) < A_HEAD_DIM
    return jnp.concatenate([jnp.where(h0, x, 0.0), jnp.where(h0, 0.0, x)], axis=0)


def _unit_lower_inverse(ns, out):
    c = ns[0].shape[0]
    row = _iota(ns[0].shape, 0)
    col = _iota(ns[0].shape, 1) & (A_HEAD_DIM - 1)
    eye = jnp.where(col == row, 1.0, 0.0)
    first = (col == row - 1) & ((row & 1) == 1)
    ts = [eye + jnp.where(first, n, 0.0) for n in ns]
    bs = 2
    while bs < c:
        sel = ((row // (2 * bs)) == (col // (2 * bs))) & (((row // bs) & 1) == 1) & (((col // bs) & 1) == 0)
        xs = [_mm(jnp.where(sel, n, 0.0), _bd(t)) for n, t in zip(ns, ts)]
        yield
        ts = [t + _mm(t, _bd(x)) for t, x in zip(ts, xs)]
        yield
        bs *= 2
    out[:] = ts


class _RwkvScratch:
    def __init__(self, s, v, ad, rd, bi, ki, dc, q, y, w, bo, uv):
        self.s, self.v, self.ad, self.rd, self.bi, self.ki, self.dc = s, v, ad, rd, bi, ki, dc
        self.q, self.y, self.w, self.bo, self.uv = q, y, w, bo, uv


def _head_pair_consts():
    li = _iota((LANES, LANES), 0) // A_HEAD_DIM
    lj = _iota((LANES, LANES), 1) // A_HEAD_DIM
    bd_mask = li == lj
    head_ones = jnp.where(bd_mask, 1.0, 0.0).astype(BF16)
    return bd_mask, jnp.concatenate([head_ones, head_ones], axis=0)


_PAIRS = [slice(p * LANES, (p + 1) * LANES) for p in range(N_PAIRS)]


def _headsum(x, head_ones2):
    return jnp.concatenate([_mm_exact_rhs(x[:, sl], head_ones2, 2) for sl in _PAIRS], axis=1)


def _rwkv_prep(r0, proj, wcomb_ref, vec_ref, sc, head_ones2):
    r, k, v, lo = proj["r"], proj["k"], proj["v"], proj["lo"]
    n = r.shape[0]
    c = CHUNK
    rows = slice(r0, r0 + n)
    w0, a0, k_k, k_a, r_k = (vec_ref[i:i + 1, :] for i in range(5))
    z = jnp.where(_iota(lo.shape, 1) < A_LORA, jnp.tanh(lo), lo)
    wa = _mm3(z, wcomb_ref[...])
    logw = -math.exp(-0.5) * _sigmoid(w0 + wa[:, 0:A_WIDTH])
    yield
    g = _chunk_cumsum(logw, c)
    a = _sigmoid(a0 + wa[:, A_WIDTH:2 * A_WIDTH])
    yield
    kk = k * k_k
    kkn = kk * lax.rsqrt(jnp.maximum(_headsum(kk * kk, head_ones2), 1e-24))
    yield
    k2 = k * (1.0 + (a - 1.0) * k_a)
    sc.v[rows, :] = v
    sc.bo[rows, :] = _headsum(r * k2 * r_k, head_ones2) * v
    yield
    e_neg = jnp.exp(-g)
    sc.bi[rows, :] = kkn * a * e_neg
    sc.ki[rows, :] = k2 * e_neg
    yield
    sc.ad[rows, :] = -kkn * jnp.exp(g - logw)
    sc.rd[rows, :] = r * jnp.exp(g)
    for j in range(n // c):
        slab = (r0 // c + j) * 8
        sc.dc[slab:slab + 8, :] = jnp.broadcast_to(jnp.exp(g[(j + 1) * c - 1:(j + 1) * c, :]), (8, A_WIDTH))
    yield


PREP_STAGES = 6


def _rwkv_local(chunk0, par, sc):
    c = CHUNK
    prow = _iota((c, LANES), 0)
    pcol = _iota((c, LANES), 1) & (A_HEAD_DIM - 1)
    strict = pcol < prow
    incl = pcol <= prow
    head0 = _iota((c, LANES), 1) < A_HEAD_DIM
    chains = [(chunk0 + j, p) for j in range(par) for p in range(N_PAIRS)]

    def rows(ci):
        return slice(ci * c, (ci + 1) * c)

    a_dec = [sc.ad[rows(ci), _PAIRS[p]] for ci, p in chains]
    r_dec = [sc.rd[rows(ci), _PAIRS[p]] for ci, p in chains]
    b_inv = [sc.bi[rows(ci), _PAIRS[p]] for ci, p in chains]
    k_inv = [sc.ki[rows(ci), _PAIRS[p]] for ci, p in chains]
    vv = [sc.v[rows(ci), _PAIRS[p]] for ci, p in chains]

    gms = [_mm_nt(jnp.concatenate([ad, rd], axis=0),
                  jnp.concatenate([jnp.where(head0, bp, 0.0), jnp.where(head0, 0.0, bp),
                                   jnp.where(head0, kp, 0.0), jnp.where(head0, 0.0, kp)], axis=0))
           for ad, rd, bp, kp in zip(a_dec, r_dec, b_inv, k_inv)]
    yield
    n_ab = [jnp.where(strict, gm[0:c, 0:LANES], 0.0) for gm in gms]
    a_ak = [jnp.where(strict, gm[0:c, LANES:2 * LANES], 0.0) for gm in gms]
    a_rb = [jnp.where(incl, gm[c:2 * c, 0:LANES], 0.0) for gm in gms]
    a_rk = [jnp.where(incl, gm[c:2 * c, LANES:2 * LANES], 0.0) for gm in gms]
    tinv = []
    yield from _unit_lower_inverse(n_ab, tinv)
    from_v = [_mm(jnp.concatenate([ak, rk], axis=0), _bd(x)) for ak, rk, x in zip(a_ak, a_rk, vv)]
    yield
    tw = [_mm(t, jnp.concatenate([_bd(ad), _bd(fv[0:c])], axis=1))
          for t, fv, ad in zip(tinv, from_v, a_dec)]
    yield
    ab = [_mm(rb, jnp.concatenate([_bd(x[:, 0:LANES]), _bd(x[:, LANES:2 * LANES])], axis=1))
          for rb, x in zip(a_rb, tw)]
    yield
    for i, (ci, p) in enumerate(chains):
        slot = ci * N_PAIRS + p
        sc.w[rows(ci), _PAIRS[p]] = tw[i][:, 0:LANES]
        sc.q[rows(ci), _PAIRS[p]] = r_dec[i] + ab[i][:, 0:LANES]
        sc.y[rows(ci), _PAIRS[p]] = ab[i][:, LANES:2 * LANES] + from_v[i][c:2 * c]
        sc.uv[slot] = jnp.concatenate([tw[i][:, LANES:2 * LANES], vv[i]], axis=0).T
    yield


LOCAL_STAGES = 15


def _rwkv_state_step(ci, sc, bd_mask):
    c = CHUNK
    rows = slice(ci * c, (ci + 1) * c)
    dc = sc.dc[ci * 8:ci * 8 + 1, :]
    sts = [sc.s[p] for p in range(N_PAIRS)]
    zeros = jnp.zeros((c, LANES), F32)
    uvs = [_mm_nt(st, jnp.concatenate([sc.w[rows, sl], zeros], axis=0)) + sc.uv[ci * N_PAIRS + p]
           for p, (st, sl) in enumerate(zip(sts, _PAIRS))]
    ys = [_mm_nt(sc.q[rows, sl], st) for st, sl in zip(sts, _PAIRS)]
    yield
    upd = [_mm(uv, jnp.concatenate([sc.bi[rows, sl], sc.ki[rows, sl]], axis=0) * dc[:, sl])
           for uv, sl in zip(uvs, _PAIRS)]
    yield
    for p, sl in enumerate(_PAIRS):
        sc.s[p] = sts[p] * dc[:, sl] + jnp.where(bd_mask, upd[p], 0.0)
        sc.y[rows, sl] = sc.y[rows, sl] + ys[p]
    yield


STATE_STAGES = 3


def _rwkv_finish(r0, n, vec_ref, sc, ga_sc, ya_sc, head_ones2):
    rows = slice(r0, r0 + n)
    ln_w = vec_ref[5:6, :]
    ln_b = vec_ref[6:7, :]
    y = sc.y[rows, :]
    inv_n = 1.0 / A_HEAD_DIM
    mean = _headsum(y, head_ones2) * inv_n
    yield
    d = y - mean
    var = _headsum(d * d, head_ones2) * inv_n
    yield
    yn = d * lax.rsqrt(var + A_GN_EPS) * ln_w + ln_b
    ya_sc[rows, :] = _bf((yn + sc.bo[rows, :]) * _silu(ga_sc[rows, :]))
    yield


def _swa_block(j, first_tile, sink_ref, bias_ref, qb_sc, kv_sc, gb_sc, yb_sc):
    h0 = _iota((2 * BLOCK, B_KV_WIDTH), 1) < B_HEAD_DIM
    heads = [(hq // 2, hq % 2, hq // B_GROUP) for hq in range(B_HEADS)]

    def ext(straight, rolled, kvh, e):
        src = straight if kvh == e else rolled
        return _bf(jnp.where(h0, src, 0.0) if e == 0 else jnp.where(h0, 0.0, src))

    rows = slice(j * BLOCK, (j + 1) * BLOCK)
    variant = jnp.where(first_tile, 0, 1) if j == 0 else 1
    kv = kv_sc[j * BLOCK:(j + 2) * BLOCK, :]
    kc = kv[:, 0:B_KV_WIDTH]
    vc = kv[:, B_KV_WIDTH:2 * B_KV_WIDTH]
    kr = pltpu.roll(kc, B_HEAD_DIM, axis=1)
    vr = pltpu.roll(vc, B_HEAD_DIM, axis=1)
    kext = {(kvh, e): ext(kc, kr, kvh, e) for _, e, kvh in heads}
    vext = {(kvh, e): ext(vc, vr, kvh, e) for _, e, kvh in heads}
    q = [qb_sc[rows, p * LANES:(p + 1) * LANES] for p in range(B_WIDTH // LANES)]
    yield
    out = [None] * B_HEADS
    for h_lo in range(0, B_HEADS, 2):
        grp = list(enumerate(heads))[h_lo:h_lo + 2]
        s = [lax.dot_general(q[p], kext[(kvh, e)], (((1,), (1,)), ((), ())), preferred_element_type=F32)
             + bias_ref[variant, hq] for hq, (p, e, kvh) in grp]
        m = [jnp.maximum(jnp.max(x, axis=-1, keepdims=True), sink_ref[hq]) for (hq, _), x in zip(grp, s)]
        yield
        pe = [jnp.exp2(x - mx) for x, mx in zip(s, m)]
        inv = [1.0 / (jnp.sum(x, axis=-1, keepdims=True) + jnp.exp2(sink_ref[hq] - mx))
               for (hq, _), x, mx in zip(grp, pe, m)]
        yield
        for (hq, (_, e, kvh)), x, rcp in zip(grp, pe, inv):
            out[hq] = jnp.dot(_bf(x), vext[(kvh, e)], preferred_element_type=F32) * rcp
        p = h_lo // 2
        sl = slice(p * LANES, (p + 1) * LANES)
        yb_sc[rows, sl] = _bf((out[h_lo] + out[h_lo + 1]) * _silu(gb_sc[rows, sl]))
        yield


SWA_STAGES = 13


def _layer0_kernel(sink_ref, x_ref, nw_ref, w_rkv, w_lora, w_ga, w_qb, w_kvb, w_gb, mu_rkv, mu_lora, mu_ga,
                   wcomb_ref, vec_ref, bias_ref, wo_a, wo_b, pnw_ref, o_ref,
                   c_rkv, c_lora, c_ga, ga_sc, qb_sc, kv_sc, gb_sc, ya_sc, yb_sc, *rwkv_scratch, par):
    ti = pl.program_id(1)
    tc = x_ref.shape[1]
    sc = _RwkvScratch(*rwkv_scratch)
    grp = par * CHUNK
    n_grp = tc // grp
    blocks_per_grp = grp // BLOCK
    bd_mask, head_ones2 = _head_pair_consts()

    @pl.when(ti == 0)
    def _():
        c_rkv[...] = jnp.zeros_like(c_rkv)
        c_lora[...] = jnp.zeros_like(c_lora)
        c_ga[...] = jnp.zeros_like(c_ga)
        sc.s[...] = jnp.zeros_like(sc.s)
        kv_sc[0:BLOCK, :] = jnp.zeros((BLOCK, 2 * B_KV_WIDTH), F32)

    def project(g, proj):
        rows = slice(g * grp, (g + 1) * grp)
        hb = _bf(_rms(x_ref[0, rows, :], nw_ref[...]))

        def shifted(w, mu, carry, lo, hi):
            p = jnp.dot(hb, w[:, lo:hi], preferred_element_type=F32)
            prev = pltpu.roll(p, 1, axis=0)
            prev = jnp.where(_iota(p.shape, 0) == 0, carry[0:1, lo:hi], prev)
            carry[0:1, lo:hi] = p[grp - 1:grp, :]
            return p + mu[:, lo:hi] * (prev - p)

        yield
        for i, name in enumerate(("r", "k", "v")):
            proj[name] = shifted(w_rkv, mu_rkv, c_rkv, i * A_WIDTH, (i + 1) * A_WIDTH)
            yield
        proj["lo"] = shifted(w_lora, mu_lora, c_lora, 0, 2 * A_LORA)
        ga_sc[rows, :] = shifted(w_ga, mu_ga, c_ga, 0, A_WIDTH)
        yield
        qb_sc[rows, :] = _bf(jnp.dot(hb, w_qb[...], preferred_element_type=F32) * (B_HEAD_DIM ** -0.5 * LOG2E))
        kv_sc[BLOCK + g * grp:BLOCK + (g + 1) * grp, :] = jnp.dot(hb, w_kvb[...], preferred_element_type=F32)
        yield
        gb_sc[rows, :] = jnp.dot(hb, w_gb[...], preferred_element_type=F32)
        yield

    project_stages = 7

    def attend(g):
        return _chain(*[_swa_block(j, ti == 0, sink_ref, bias_ref, qb_sc, kv_sc, gb_sc, yb_sc)
                        for j in range(g * blocks_per_grp, (g + 1) * blocks_per_grp)])

    def state(g):
        return _chain(*[_rwkv_state_step(ci, sc, bd_mask) for ci in range(g * par, (g + 1) * par)])

    def finish(g):
        rows = slice(g * grp, (g + 1) * grp)
        yield from _rwkv_finish(g * grp, grp, vec_ref, sc, ga_sc, ya_sc, head_ones2)
        y = (jnp.dot(ya_sc[rows, :], wo_a[...], preferred_element_type=F32)
             + jnp.dot(yb_sc[rows, :], wo_b[...], preferred_element_type=F32))
        yield
        o_ref[0, rows, :] = x_ref[0, rows, :] + _rms(y, pnw_ref[...])
        yield

    finish_stages = 5

    def on(gen_fn, g):
        return gen_fn(g) if 0 <= g < n_grp else None

    projs = [dict() for _ in range(n_grp)]
    _run_interleaved((project(0, projs[0]), project_stages))
    for g in range(n_grp + 3):
        _run_interleaved(
            (project(g + 1, projs[g + 1]) if g + 1 < n_grp else None, project_stages),
            (_rwkv_prep(g * grp, projs[g], wcomb_ref, vec_ref, sc, head_ones2) if g < n_grp else None, PREP_STAGES),
            (_rwkv_local((g - 1) * par, par, sc) if 0 <= g - 1 < n_grp else None, LOCAL_STAGES),
            (on(attend, g - 1), SWA_STAGES * blocks_per_grp),
            (on(state, g - 2), STATE_STAGES * par),
            (on(finish, g - 3), finish_stages))
        if g - 1 == n_grp - 1:
            kv_sc[0:BLOCK, :] = kv_sc[tc:tc + BLOCK, :]


def _layer0(sinks, x, nw, ws, mus, wcomb, vecs, bias, wo_a, wo_b, pnw, tc):
    bsz, t, d = x.shape
    n_chunks = tc // CHUNK
    par = math.gcd(A_PAR, n_chunks)
    full = lambda a: pl.BlockSpec(a.shape, lambda b, i: (0,) * a.ndim)
    tok = pl.BlockSpec((1, tc, d), lambda b, i: (b, i, 0))
    widths = [w.shape[1] for w in ws]
    tile = pltpu.VMEM((tc, A_WIDTH), F32)
    tile16 = pltpu.VMEM((tc, A_WIDTH), BF16)
    scratch = [pltpu.VMEM((8, widths[0]), F32), pltpu.VMEM((8, widths[1]), F32), pltpu.VMEM((8, widths[2]), F32),
               tile,
               tile16,
               pltpu.VMEM((tc + BLOCK, 2 * B_KV_WIDTH), F32),
               tile,
               tile16, tile16,
               pltpu.VMEM((N_PAIRS, LANES, LANES), F32),
               tile, tile, tile, tile, tile,
               pltpu.VMEM((n_chunks * 8, A_WIDTH), F32),
               tile, tile, tile, tile,
               pltpu.VMEM((n_chunks * N_PAIRS, LANES, LANES), F32)]
    return pl.pallas_call(
        functools.partial(_layer0_kernel, par=par), grid=(bsz, t // tc),
        in_specs=[pl.BlockSpec(memory_space=pltpu.SMEM), tok, full(nw)] + [full(w) for w in ws]
                 + [full(m) for m in mus] + [full(wcomb), full(vecs), full(bias), full(wo_a), full(wo_b), full(pnw)],
        out_specs=tok, out_shape=jax.ShapeDtypeStruct((bsz, t, d), F32),
        scratch_shapes=scratch,
        compiler_params=_cparams(("parallel", "arbitrary")), name="layer0",
    )(sinks, x, nw, *ws, *mus, wcomb, vecs, bias, wo_a, wo_b, pnw)


def _layer1_kernel(x_ref, nw_ref, w_ref, lb_ref, cw_ref, wo_ref, pnw_ref, o_ref,
                   s_ref, qd_sc, ki_sc, v_sc, g_sc, dc_sc, o_sc, y_sc, *, par, group):
    ti = pl.program_id(1)

    @pl.when(ti == 0)
    def _():
        s_ref[...] = jnp.zeros_like(s_ref)

    tm = x_ref.shape[1]
    c = CHUNK
    grp = group * c
    n_grp = tm // grp
    l0 = lb_ref[0:1, :]
    l1 = lb_ref[1:2, :]
    m = jnp.maximum(l0, l1)
    e0 = jnp.exp(l0 - m)
    e1 = jnp.exp(l1 - m)
    s0 = e0 / (e0 + e1)
    s1 = e1 / (e0 + e1)
    lb = (s0 + s1) - s0
    causal = _iota((c, c), 1) <= _iota((c, c), 0)
    heads = [slice(hd * C_DIM, (hd + 1) * C_DIM) for hd in range(C_HEADS)]

    def project(g):
        rows = slice(g * grp, (g + 1) * grp)
        p = jnp.dot(_bf(_rms(x_ref[0, rows, :], nw_ref[...])), w_ref[...], preferred_element_type=F32)
        v_sc[rows, :] = _bf(p[:, 2 * C_WIDTH:3 * C_WIDTH])
        g_sc[rows, :] = p[:, 3 * C_WIDTH:4 * C_WIDTH]
        forget = lb + (1.0 - lb) * _sigmoid(p[:, C_WIDTH:2 * C_WIDTH])
        gc = _chunk_cumsum(jnp.log(forget), c)
        qd_sc[rows, :] = _bf(_silu(p[:, 0:C_WIDTH]) * jnp.exp(gc))
        ki_sc[rows, :] = (1.0 - forget) * jnp.exp(-gc)
        for j in range(group):
            slab = (g * group + j) * 8
            dc_sc[slab:slab + 8, :] = jnp.broadcast_to(jnp.exp(gc[(j + 1) * c - 1:(j + 1) * c, :]), (8, C_WIDTH))

    def chunks(chunk0, sts):
        chains = [(chunk0 + j, hd) for j in range(par) for hd in range(C_HEADS)]

        def rows(ci):
            return slice(ci * c, (ci + 1) * c)

        q_dec = [qd_sc[rows(ci), heads[hd]] for ci, hd in chains]
        k_inv = [ki_sc[rows(ci), heads[hd]] for ci, hd in chains]
        vv = [v_sc[rows(ci), heads[hd]] for ci, hd in chains]
        dcs = [dc_sc[ci * 8:ci * 8 + 1, heads[hd]] for ci, hd in chains]
        sc = [jnp.where(causal, _mm_nt(qd, ki), 0.0) for qd, ki in zip(q_dec, k_inv)]
        upd = [_mm(vx.astype(F32).T, ki * dc) for vx, ki, dc in zip(vv, k_inv, dcs)]
        intra = [_mm(s, vx) for s, vx in zip(sc, vv)]
        for i, (ci, hd) in enumerate(chains):
            o_sc[rows(ci), heads[hd]] = intra[i] + _mm_nt(q_dec[i], sts[hd])
            sts[hd] = sts[hd] * dcs[i] + upd[i]

    def finish(g):
        rows = slice(g * grp, (g + 1) * grp)
        for sl in heads:
            y_sc[rows, sl] = _bf(_rms(o_sc[rows, sl], cw_ref[...]) * _silu(g_sc[rows, sl]))
        y = jnp.dot(y_sc[rows, :], wo_ref[...], preferred_element_type=F32)
        o_ref[0, rows, :] = x_ref[0, rows, :] + _rms(y, pnw_ref[...])

    sts = [s_ref[hd] for hd in range(C_HEADS)]
    project(0)
    for g in range(n_grp):
        if g + 1 < n_grp:
            project(g + 1)
        for ci in range(g * group, (g + 1) * group, par):
            chunks(ci, sts)
        finish(g)
    for hd in range(C_HEADS):
        s_ref[hd] = sts[hd]


def _layer1(x, nw, w, lbs, cw, wo, pnw, tm):
    bsz, t, d = x.shape
    group = math.gcd(C_GROUP, tm // CHUNK)
    par = math.gcd(C_PAR, group)
    full = lambda a: pl.BlockSpec(a.shape, lambda b, i: (0,) * a.ndim)
    once = lambda a: pl.BlockSpec(a.shape, lambda b, i: (0,) * a.ndim, pipeline_mode=pl.Buffered(1))
    tok = pl.BlockSpec((1, tm, d), lambda b, i: (b, i, 0))
    act16 = pltpu.VMEM((tm, C_WIDTH), BF16)
    act32 = pltpu.VMEM((tm, C_WIDTH), F32)
    return pl.pallas_call(
        functools.partial(_layer1_kernel, par=par, group=group), grid=(bsz, t // tm),
        in_specs=[tok, full(nw), once(w), full(lbs), full(cw), once(wo), full(pnw)],
        out_specs=tok, out_shape=jax.ShapeDtypeStruct((bsz, t, d), F32),
        scratch_shapes=[pltpu.VMEM((C_HEADS, C_DIM, C_DIM), F32),
                        act16, act32, act16, act32,
                        pltpu.VMEM((tm // CHUNK * 8, C_WIDTH), F32),
                        act32, act16],
        compiler_params=_cparams(("parallel", "arbitrary")), name="layer1",
    )(x, nw, w, lbs, cw, wo, pnw)


def _t5_bucket(dist):
    max_exact = REL_BUCKETS // 2
    d = jnp.maximum(dist, 0)
    large = max_exact + jnp.floor(jnp.log(jnp.maximum(d, 1).astype(F32) / max_exact)
                                  / math.log(REL_MAX_DIST / max_exact)
                                  * (REL_BUCKETS - max_exact)).astype(jnp.int32)
    large = jnp.minimum(large, REL_BUCKETS - 1)
    return jnp.where(d < max_exact, d, large)


def _tile(t, want):
    return want if t % want == 0 else BLOCK


def kernel(x, rel_bias_table, lower_bounds, l0_pre_norm, l0_post_norm, l0_w_in, l0_w_out, a_mu, a_w0, a_w2, a_a0, a_a2, a_k_k, a_k_a, a_r_k, a_ln_w, a_ln_b, b_sinks, l1_pre_norm, l1_post_norm, l1_w_in, l1_w_out, c_norm_w):
    bsz, t, d = x.shape
    assert d == D_MODEL and t % BLOCK == 0
    row = lambda v: v.reshape(1, -1).astype(F32)

    c_rkv, c_lora, c_ga = 3 * A_WIDTH, 3 * A_WIDTH + 2 * A_LORA, 4 * A_WIDTH + 2 * A_LORA
    c_q, c_kv, c_gb = c_ga + B_WIDTH, c_ga + B_WIDTH + 2 * B_KV_WIDTH, c_ga + 2 * B_WIDTH + 2 * B_KV_WIDTH
    wb = _bf(l0_w_in)
    ws = [wb[:, 0:c_rkv], wb[:, c_rkv:c_lora], wb[:, c_lora:c_ga], wb[:, c_ga:c_q], wb[:, c_q:c_kv], wb[:, c_kv:c_gb]]
    mus = [row(a_mu[0:c_rkv]), row(a_mu[c_rkv:c_lora]), row(a_mu[c_lora:c_ga])]
    zeros = jnp.zeros((A_LORA, A_WIDTH), F32)
    wcomb = jnp.concatenate([jnp.concatenate([a_w2, zeros], axis=1), jnp.concatenate([zeros, a_a2], axis=1)], axis=0)
    vecs = jnp.stack([a_w0, a_a0, a_k_k, a_k_a, a_r_k.reshape(-1), a_ln_w, a_ln_b, jnp.zeros_like(a_w0)]).astype(F32)
    qi = jnp.arange(BLOCK)[:, None]
    kj = jnp.arange(2 * BLOCK)[None, :]
    dist = qi + BLOCK - kj
    onehot = (_t5_bucket(dist)[:, :, None] == jnp.arange(REL_BUCKETS)).astype(F32)
    bias = jnp.einsum("qkb,bh->hqk", onehot, rel_bias_table.astype(F32), precision=lax.Precision.HIGHEST)
    in_window = (dist >= 0) & (dist < WINDOW)
    bias = bias * LOG2E
    bias = jnp.stack([jnp.where(in_window & (kj >= BLOCK), bias, NEG_INF), jnp.where(in_window, bias, NEG_INF)])
    w_out0 = _bf(l0_w_out)
    w_in1 = _bf(l1_w_in)
    w_out1 = _bf(l1_w_out)

    x1 = _layer0(b_sinks.astype(F32) * LOG2E, x, row(l0_pre_norm), ws, mus, wcomb, vecs, bias,
                 w_out0[0:A_WIDTH], w_out0[A_WIDTH:], row(l0_post_norm), _tile(t, 512))
    return _layer1(x1, row(l1_pre_norm), w_in1, lower_bounds.astype(F32), row(c_norm_w), w_out1, row(l1_post_norm),
                   _tile(t, 1024))
```
